```python
import math
import jax, jax.numpy as jnp
from jax import lax
import numpy as np

D_MODEL = 1024
BATCH = 8
SEQ = 2048
DEPTH = 1

MEM_LEN = 256
CONV_CH = D_MODEL
CONV_K = 31
DIFF_HEADS = D_MODEL // 128
DIFF_HEAD_DIM = 64
DIFF_V_DIM = 2 * DIFF_HEAD_DIM
DIFF_WIDTH = DIFF_HEADS * DIFF_V_DIM
X_HEADS = 4
X_HEAD_DIM = D_MODEL // X_HEADS
X_WIDTH = X_HEADS * X_HEAD_DIM
N_BRANCH = 3
Q_BLOCK = 128

IN_SIZES = (
    2 * CONV_CH,
    CONV_CH,
    DIFF_WIDTH,
    DIFF_WIDTH,
    DIFF_WIDTH,
    DIFF_WIDTH,
    X_WIDTH,
    X_WIDTH,
    N_BRANCH * D_MODEL,
)
IN_COLS = sum(IN_SIZES)
IN_SPLITS = tuple(int(c) for c in np.cumsum(IN_SIZES)[:-1])

kernel_name = "hybrid_conformer_diffattn_memxattn_gated"


def rms_norm(x, g, eps=1e-6):
    xf = x.astype(jnp.float32)
    y = xf * lax.rsqrt(jnp.mean(xf * xf, axis=-1, keepdims=True) + eps)
    return (y * g.astype(jnp.float32)).astype(x.dtype)


def layer_norm(x, g, b, eps=1e-5):
    xf = x.astype(jnp.float32)
    mu = jnp.mean(xf, axis=-1, keepdims=True)
    xc = xf - mu
    y = xc * lax.rsqrt(jnp.mean(xc * xc, axis=-1, keepdims=True) + eps)
    return (y * g.astype(jnp.float32) + b.astype(jnp.float32)).astype(x.dtype)


def alibi_slopes(n_heads):
    return 2.0 ** (-8.0 * jnp.arange(1, n_heads + 1, dtype=jnp.float32) / n_heads)


def lambda_init_for(layer_idx):
    return 0.8 - 0.6 * math.exp(-0.3 * layer_idx)


def conformer_conv_branch(a_glu, gate, dw, dw_b, ln_g, ln_b, w_proj):
    a, b = jnp.split(a_glu, 2, axis=-1)
    u = a * jax.nn.sigmoid(b)
    u = lax.conv_general_dilated(
        u, dw[:, None, :], window_strides=(1,), padding=[(CONV_K - 1, 0)],
        dimension_numbers=("NWC", "WIO", "NWC"), feature_group_count=CONV_CH) + dw_b
    u = layer_norm(u, ln_g, ln_b)
    u = jax.nn.silu(u) * jax.nn.silu(gate)
    return u @ w_proj


def differential_attention_branch(q, k, v, gate, qn_g, kn_g, lq1, lk1, lq2, lk2,
                                  subln_g, w_proj, lambda_init):
    B, S = q.shape[0], q.shape[1]
    H, d = DIFF_HEADS, DIFF_HEAD_DIM
    nb = S // Q_BLOCK
    q = rms_norm(q.reshape(B, S, H, 2, d), qn_g)
    k = rms_norm(k.reshape(B, S, H, 2, d), kn_g)
    kt = k.transpose(0, 2, 3, 1, 4)
    vt = v.reshape(B, S, H, DIFF_V_DIM).transpose(0, 2, 1, 3)
    qb = q.reshape(B, nb, Q_BLOCK, H, 2, d).transpose(1, 0, 3, 4, 2, 5)
    lam = (jnp.exp(jnp.sum(lq1.astype(jnp.float32) * lk1.astype(jnp.float32)))
           - jnp.exp(jnp.sum(lq2.astype(jnp.float32) * lk2.astype(jnp.float32)))
           + lambda_init)
    slopes = alibi_slopes(H)
    scale = DIFF_HEAD_DIM ** -0.5
    kpos = jnp.arange(S)

    def one_block(args):
        qblk, t0 = args
        tpos = t0 + jnp.arange(Q_BLOCK)
        dist = (tpos[:, None] - kpos[None, :]).astype(jnp.float32)
        bias = -slopes[:, None, None] * dist
        s = jnp.einsum("bhmqd,bhmkd->bhmqk", qblk, kt).astype(jnp.float32) * scale
        s = s + bias[None, :, None]
        s = jnp.where(dist >= 0, s, -jnp.inf)
        p = jax.nn.softmax(s, axis=-1)
        a = (p[:, :, 0] - lam * p[:, :, 1]).astype(vt.dtype)
        return jnp.einsum("bhqk,bhkd->bhqd", a, vt)

    o = lax.map(one_block, (qb, jnp.arange(nb) * Q_BLOCK))
    o = o.transpose(1, 0, 3, 2, 4).reshape(B, S, H, DIFF_V_DIM)
    o = rms_norm(o, subln_g) * (1.0 - lambda_init)
    o = o.reshape(B, S, DIFF_WIDTH) * jax.nn.silu(gate)
    return o @ w_proj


def memory_cross_attention_branch(q, gate, mem_h, w_mem_kv, qn_g, kn_g, w_proj):
    B, S = q.shape[0], q.shape[1]
    M = mem_h.shape[1]
    k, v = jnp.split(mem_h @ w_mem_kv, 2, axis=-1)
    q = rms_norm(q.reshape(B, S, X_HEADS, X_HEAD_DIM), qn_g)
    k = rms_norm(k.reshape(B, M, X_HEADS, X_HEAD_DIM), kn_g)
    v = v.reshape(B, M, X_HEADS, X_HEAD_DIM)
    s = jnp.einsum("bshd,bmhd->bhsm", q, k).astype(jnp.float32) * (X_HEAD_DIM ** -0.5)
    p = jax.nn.softmax(s, axis=-1).astype(v.dtype)
    o = jnp.einsum("bhsm,bmhd->bshd", p, v).reshape(B, S, X_WIDTH) * jax.nn.silu(gate)
    return o @ w_proj


def setup_inputs(seed: int = 0) -> dict:
    key = jax.random.key(seed)
    ks = jax.random.split(key, 24)
    f32 = jnp.float32
    L, D = DEPTH, D_MODEL

    def nrm(k, shape, scale):
        return jax.random.normal(k, shape, f32) * scale

    def gain(k, shape):
        return 1.0 + 0.05 * jax.random.normal(k, shape, f32)

    return {
        "x": jax.random.normal(ks[0], (BATCH, SEQ, D), f32),
        "mem": jax.random.normal(ks[1], (BATCH, MEM_LEN, D), f32),
        "norm_g": gain(ks[2], (L, D)),
        "mem_norm_g": gain(ks[3], (L, D)),
        "w_in": nrm(ks[4], (L, D, IN_COLS), D ** -0.5),
        "conv_dw": nrm(ks[5], (L, CONV_K, CONV_CH), CONV_K ** -0.5),
        "conv_dw_b": nrm(ks[6], (L, CONV_CH), 0.02),
        "conv_ln_g": gain(ks[7], (L, CONV_CH)),
        "conv_ln_b": nrm(ks[8], (L, CONV_CH), 0.02),
        "w_conv_proj": nrm(ks[9], (L, CONV_CH, D), CONV_CH ** -0.5),
        "diff_qn_g": gain(ks[10], (L, DIFF_HEAD_DIM)),
        "diff_kn_g": gain(ks[11], (L, DIFF_HEAD_DIM)),
        "lambda_q1": nrm(ks[12], (L, DIFF_HEAD_DIM), 0.1),
        "lambda_k1": nrm(ks[13], (L, DIFF_HEAD_DIM), 0.1),
        "lambda_q2": nrm(ks[14], (L, DIFF_HEAD_DIM), 0.1),
        "lambda_k2": nrm(ks[15], (L, DIFF_HEAD_DIM), 0.1),
        "diff_subln_g": gain(ks[16], (L, DIFF_V_DIM)),
        "w_diff_proj": nrm(ks[17], (L, DIFF_WIDTH, D), DIFF_WIDTH ** -0.5),
        "w_mem_kv": nrm(ks[18], (L, D, 2 * X_WIDTH), D ** -0.5),
        "x_qn_g": gain(ks[19], (L, X_HEAD_DIM)),
        "x_kn_g": gain(ks[20], (L, X_HEAD_DIM)),
        "w_x_proj": nrm(ks[21], (L, X_WIDTH, D), X_WIDTH ** -0.5),
        "w_out": nrm(ks[22], (L, D, D), D ** -0.5),
    }


def reference(x, mem, norm_g, mem_norm_g, w_in, conv_dw, conv_dw_b, conv_ln_g, conv_ln_b,
              w_conv_proj, diff_qn_g, diff_kn_g, lambda_q1, lambda_k1, lambda_q2, lambda_k2,
              diff_subln_g, w_diff_proj, w_mem_kv, x_qn_g, x_kn_g, w_x_proj, w_out):
    B, S, D = x.shape
    for l in range(DEPTH):
        h = rms_norm(x, norm_g[l])
        mem_h = rms_norm(mem, mem_norm_g[l])
        (c_glu, c_gate, d_q, d_k, d_v, d_gate, x_q, x_gate, merge) = jnp.split(
            h @ w_in[l], IN_SPLITS, axis=-1)
        y_conv = conformer_conv_branch(c_glu, c_gate, conv_dw[l], conv_dw_b[l],
                                       conv_ln_g[l], conv_ln_b[l], w_conv_proj[l])
        y_diff = differential_attention_branch(
            d_q, d_k, d_v, d_gate, diff_qn_g[l], diff_kn_g[l], lambda_q1[l], lambda_k1[l],
            lambda_q2[l], lambda_k2[l], diff_subln_g[l], w_diff_proj[l], lambda_init_for(l))
        y_mem = memory_cross_attention_branch(x_q, x_gate, mem_h, w_mem_kv[l],
                                              x_qn_g[l], x_kn_g[l], w_x_proj[l])
        g = jax.nn.sigmoid(merge.reshape(B, S, N_BRANCH, D))
        y = g[:, :, 0] * y_conv + g[:, :, 1] * y_diff + g[:, :, 2] * y_mem
        x = x + y @ w_out[l]
    return x
```

```python
import functools
import math

import jax
import jax.numpy as jnp
from jax import lax
from jax.experimental import pallas as pl
from jax.experimental.pallas import tpu as pltpu

CONV_K = 31
DIFF_HEAD_DIM = 64
DIFF_V_DIM = 2 * DIFF_HEAD_DIM
X_HEADS = 4
N_BRANCH = 3
RMS_EPS = 1e-6
LN_EPS = 1e-5
MASK_VALUE = -1e30

V7X_LANES = 128
V7X_MXU_DIM = 256
V7X_VMEM_LIMIT_BYTES = 56 * 1024 * 1024

BF16 = jnp.bfloat16
F32 = jnp.float32


def _cparams(sem):
    return pltpu.CompilerParams(dimension_semantics=sem, vmem_limit_bytes=V7X_VMEM_LIMIT_BYTES)


def _sigmoid(x):
    return 1.0 / (1.0 + jnp.exp(-x))


def _silu(x):
    return x * _sigmoid(x)


def _nt_dot(a, b):
    return lax.dot_general(a, b, (((1,), (1,)), ((), ())), preferred_element_type=F32)


def _rmsnorm_kernel(x_ref, g_ref, o_ref):
    x = x_ref[...]
    y = x * lax.rsqrt(jnp.mean(x * x, axis=-1, keepdims=True) + RMS_EPS)
    o_ref[...] = (y * g_ref[...]).astype(o_ref.dtype)


def _rmsnorm(x2d, g, tm):
    n, d = x2d.shape
    return pl.pallas_call(
        _rmsnorm_kernel,
        grid=(n // tm,),
        in_specs=[pl.BlockSpec((tm, d), lambda i: (i, 0)),
                  pl.BlockSpec((1, d), lambda i: (0, 0))],
        out_specs=pl.BlockSpec((tm, d), lambda i: (i, 0)),
        out_shape=jax.ShapeDtypeStruct((n, d), BF16),
        compiler_params=_cparams(("parallel",)),
        name="rmsnorm",
    )(x2d, g.reshape(1, d))


def _group_rms(x, gmat, group, gain):
    x2 = x * x
    hi = x2.astype(BF16)
    lo = (x2 - hi.astype(F32)).astype(BF16)
    n = x.shape[1]
    parts = []
    for c in range(n // V7X_MXU_DIM):
        sl = slice(c * V7X_MXU_DIM, (c + 1) * V7X_MXU_DIM)
        parts.append(jnp.dot(hi[:, sl], gmat, preferred_element_type=F32)
                     + jnp.dot(lo[:, sl], gmat, preferred_element_type=F32))
    ss = jnp.concatenate(parts, axis=1)
    return x * lax.rsqrt(ss * (1.0 / group) + RMS_EPS) * gain


def _glu_kernel(h_ref, wa_ref, wb_ref, o_ref):
    h = h_ref[...]
    a = jnp.dot(h, wa_ref[...], preferred_element_type=F32)
    b = jnp.dot(h, wb_ref[...], preferred_element_type=F32)
    o_ref[...] = (a * _sigmoid(b)).astype(o_ref.dtype)


def _glu_proj(hb, w, tm, tn, ncol):
    n, d = hb.shape
    nb = ncol // tn
    return pl.pallas_call(
        _glu_kernel,
        grid=(n // tm, nb),
        in_specs=[pl.BlockSpec((tm, d), lambda i, j: (i, 0)),
                  pl.BlockSpec((d, tn), lambda i, j: (0, j)),
                  pl.BlockSpec((d, tn), lambda i, j: (0, nb + j))],
        out_specs=pl.BlockSpec((tm, tn), lambda i, j: (i, j)),
        out_shape=jax.ShapeDtypeStruct((n, ncol), BF16),
        compiler_params=_cparams(("parallel", "arbitrary")),
        name="glu_proj",
    )(hb, w, w)


def _seg_proj_kernel(h_ref, w_ref, gain_ref, g64_ref, g256_ref, o_ref, *, modes):
    j = pl.program_id(1)
    acc = jnp.dot(h_ref[...], w_ref[...], preferred_element_type=F32)

    def emit(fn, blocks):
        if not blocks:
            return
        cond = functools.reduce(jnp.logical_or, [j == b for b in blocks])

        @pl.when(cond)
        def _():
            o_ref[...] = fn(acc).astype(o_ref.dtype)

    emit(_silu, modes["silu"])
    emit(_sigmoid, modes["sigmoid"])
    emit(lambda a: a, modes["none"])
    emit(lambda a: _group_rms(a, g64_ref[...], DIFF_HEAD_DIM, gain_ref[...]), modes["norm64"])
    emit(lambda a: _group_rms(a, g256_ref[...], V7X_MXU_DIM, gain_ref[...]), modes["norm256"])


def _seg_proj(hb, w, col_block0, gains, g64, g256, modes, tm, tn):
    n, d = hb.shape
    nblk = gains.shape[0]
    return pl.pallas_call(
        functools.partial(_seg_proj_kernel, modes=modes),
        grid=(n // tm, nblk),
        in_specs=[pl.BlockSpec((tm, d), lambda i, j: (i, 0)),
                  pl.BlockSpec((d, tn), lambda i, j: (0, col_block0 + j)),
                  pl.BlockSpec((None, 1, tn), lambda i, j: (j, 0, 0)),
                  pl.BlockSpec((V7X_MXU_DIM, V7X_MXU_DIM), lambda i, j: (0, 0)),
                  pl.BlockSpec((V7X_MXU_DIM, V7X_MXU_DIM), lambda i, j: (0, 0))],
        out_specs=pl.BlockSpec((tm, tn), lambda i, j: (i, j)),
        out_shape=jax.ShapeDtypeStruct((n, nblk * tn), BF16),
        compiler_params=_cparams(("parallel", "arbitrary")),
        name="seg_proj",
    )(hb, w, gains, g64, g256)


def _group_ones(group):
    r = jnp.arange(V7X_MXU_DIM) // group
    return (r[:, None] == r[None, :]).astype(BF16)


CONV_HALO = 32
CONV_ROWS = 32
CONV_COLS = 512


def _conv_kernel(u_ref, gate_ref, dw_ref, dwb_ref, lng_ref, lnb_ref, o_ref, win_ref, y_ref, *, ts):
    j = pl.program_id(1)
    t0 = pl.multiple_of(j * ts, ts)
    d = u_ref.shape[1]

    @pl.when(j == 0)
    def _():
        win_ref[0:CONV_HALO, :] = jnp.zeros((CONV_HALO, d), F32)

    @pl.when(j > 0)
    def _():
        win_ref[0:CONV_HALO, :] = u_ref[pl.ds(t0 - CONV_HALO, CONV_HALO), :].astype(F32)

    win_ref[CONV_HALO:CONV_HALO + ts, :] = u_ref[pl.ds(t0, ts), :].astype(F32)

    first = CONV_HALO - (CONV_K - 1)

    def row_chunk(r, carry):
        r0 = pl.multiple_of(r * CONV_ROWS, CONV_ROWS)
        for c in range(d // CONV_COLS):
            cs = slice(c * CONV_COLS, (c + 1) * CONV_COLS)
            acc = jnp.zeros((CONV_ROWS, CONV_COLS), F32)
            w = win_ref[pl.ds(r0, CONV_ROWS + CONV_HALO), cs]
            for k in range(CONV_K):
                acc = acc + w[first + k:first + k + CONV_ROWS, :] * dw_ref[k:k + 1, cs]
            y_ref[pl.ds(r0, CONV_ROWS), cs] = acc + dwb_ref[:, cs]
        return carry

    lax.fori_loop(0, ts // CONV_ROWS, row_chunk, 0)

    y = y_ref[...]
    mu = jnp.mean(y, axis=-1, keepdims=True)
    yc = y - mu
    yn = yc * lax.rsqrt(jnp.mean(yc * yc, axis=-1, keepdims=True) + LN_EPS)
    yn = yn * lng_ref[...] + lnb_ref[...]
    o_ref[...] = (_silu(yn) * gate_ref[...].astype(F32)).astype(o_ref.dtype)


def _conv_branch(u, gates, gate_block, dw, dwb, lng, lnb, batch, seq, ts):
    n, d = u.shape
    nt = seq // ts
    return pl.pallas_call(
        functools.partial(_conv_kernel, ts=ts),
        grid=(batch, nt),
        in_specs=[pl.BlockSpec((seq, d), lambda b, j: (b, 0)),
                  pl.BlockSpec((ts, d), lambda b, j: (b * nt + j, gate_block)),
                  pl.BlockSpec((CONV_K, d), lambda b, j: (0, 0)),
                  pl.BlockSpec((1, d), lambda b, j: (0, 0)),
                  pl.BlockSpec((1, d), lambda b, j: (0, 0)),
                  pl.BlockSpec((1, d), lambda b, j: (0, 0))],
        out_specs=pl.BlockSpec((ts, d), lambda b, j: (b * nt + j, 0)),
        out_shape=jax.ShapeDtypeStruct((n, d), BF16),
        scratch_shapes=[pltpu.VMEM((CONV_HALO + ts, d), F32), pltpu.VMEM((ts, d), F32)],
        compiler_params=_cparams(("parallel", "arbitrary")),
        name="conv_branch",
    )(u, gates, dw, dwb.reshape(1, d), lng.reshape(1, d), lnb.reshape(1, d))


def _diff_attn_kernel(slopes_ref, q_ref, k_ref, v_ref, gate_ref, lam_ref, subg_ref, o_ref,
                      bias_ref, m_ref, l_ref, acc_ref, *, tq, lambda_init):
    h = pl.program_id(1)
    i = pl.program_id(2)
    slope = slopes_ref[h]
    q = q_ref[...]

    rows = lax.broadcasted_iota(jnp.int32, (tq, tq), 0)
    cols = lax.broadcasted_iota(jnp.int32, (tq, tq), 1)
    bias_ref[...] = (cols - rows).astype(F32) * slope
    m_ref[...] = jnp.full(m_ref.shape, MASK_VALUE, F32)
    l_ref[...] = jnp.zeros(l_ref.shape, F32)
    acc_ref[...] = jnp.zeros(acc_ref.shape, F32)

    lane = lax.broadcasted_iota(jnp.int32, (tq, DIFF_V_DIM), 1)
    first_map = lane < DIFF_HEAD_DIM

    def kv_block(jb, masked):
        s0 = pl.multiple_of(jb * tq, tq)
        k = k_ref[pl.ds(s0, tq), :]
        v = v_ref[pl.ds(s0, tq), :]
        zero = jnp.zeros_like(k)
        kmaps = (jnp.where(first_map, k, zero), jnp.where(first_map, zero, k))
        bias = bias_ref[...] + slope * ((jb - i) * tq).astype(F32)
        for mp in range(2):
            s = _nt_dot(q, kmaps[mp]) + bias
            if masked:
                s = jnp.where(cols <= rows, s, MASK_VALUE)
            m_old = m_ref[mp]
            m_new = jnp.maximum(m_old, jnp.max(s, axis=-1, keepdims=True))
            alpha = jnp.exp(m_old - m_new)
            p = jnp.exp(s - m_new)
            l_ref[mp] = alpha * l_ref[mp] + jnp.sum(p, axis=-1, keepdims=True)
            acc_ref[mp] = alpha * acc_ref[mp] + jnp.dot(p.astype(v.dtype), v, preferred_element_type=F32)
            m_ref[mp] = m_new

    def body(jb, carry):
        kv_block(jb, False)
        return carry

    lax.fori_loop(0, i, body, 0)
    kv_block(i, True)

    lam_v = lam_ref[...]
    lam = (jnp.exp(jnp.sum(lam_v[0:1] * lam_v[1:2], axis=-1, keepdims=True))
           - jnp.exp(jnp.sum(lam_v[2:3] * lam_v[3:4], axis=-1, keepdims=True)) + lambda_init)
    o = acc_ref[0] / l_ref[0] - lam * (acc_ref[1] / l_ref[1])
    o = o * lax.rsqrt(jnp.mean(o * o, axis=-1, keepdims=True) + RMS_EPS) * subg_ref[...]
    o = o * (1.0 - lambda_init)
    o_ref[...] = (o * gate_ref[...].astype(F32)).astype(o_ref.dtype)


def _diff_attention(acts, blocks, slopes, lam_vecs, subg, batch, seq, heads, tq, lambda_init):
    n = acts.shape[0]
    nq = seq // tq
    qb, kb, vb, gb = (b * heads for b in blocks)
    kernel = functools.partial(_diff_attn_kernel, tq=tq, lambda_init=lambda_init)
    return pl.pallas_call(
        kernel,
        grid=(batch, heads, nq),
        in_specs=[pl.BlockSpec(memory_space=pltpu.SMEM),
                  pl.BlockSpec((tq, DIFF_V_DIM), lambda b, h, i: (b * nq + i, qb + h)),
                  pl.BlockSpec((seq, DIFF_V_DIM), lambda b, h, i: (b, kb + h)),
                  pl.BlockSpec((seq, DIFF_V_DIM), lambda b, h, i: (b, vb + h)),
                  pl.BlockSpec((tq, DIFF_V_DIM), lambda b, h, i: (b * nq + i, gb + h)),
                  pl.BlockSpec((4, DIFF_HEAD_DIM), lambda b, h, i: (0, 0)),
                  pl.BlockSpec((1, DIFF_V_DIM), lambda b, h, i: (0, 0))],
        out_specs=pl.BlockSpec((tq, DIFF_V_DIM), lambda b, h, i: (b * nq + i, h)),
        out_shape=jax.ShapeDtypeStruct((n, heads * DIFF_V_DIM), BF16),
        scratch_shapes=[pltpu.VMEM((tq, tq), F32),
                        pltpu.VMEM((2, tq, 1), F32),
                        pltpu.VMEM((2, tq, 1), F32),
                        pltpu.VMEM((2, tq, DIFF_V_DIM), F32)],
        compiler_params=_cparams(("parallel", "parallel", "arbitrary")),
        name="diff_attention",
    )(slopes, acts, acts, acts, acts, lam_vecs, subg.reshape(1, DIFF_V_DIM))


def _xattn_kernel(q_ref, k_ref, v_ref, gate_ref, o_ref, *, heads):
    hd = q_ref.shape[1] // heads
    for h in range(heads):
        sl = slice(h * hd, (h + 1) * hd)
        s = _nt_dot(q_ref[:, sl], k_ref[:, sl])
        m = jnp.max(s, axis=-1, keepdims=True)
        p = jnp.exp(s - m)
        l = jnp.sum(p, axis=-1, keepdims=True)
        o = jnp.dot(p.astype(BF16), v_ref[:, sl], preferred_element_type=F32) / l
        o_ref[:, sl] = (o * gate_ref[:, sl].astype(F32)).astype(o_ref.dtype)


def _cross_attention(acts, q_block, gate_block, mem_kv, batch, seq, mem_len, heads, tq):
    n = acts.shape[0]
    d = mem_kv.shape[1] // 2
    nq = seq // tq
    return pl.pallas_call(
        functools.partial(_xattn_kernel, heads=heads),
        grid=(batch, nq),
        in_specs=[pl.BlockSpec((tq, d), lambda b, i: (b * nq + i, q_block)),
                  pl.BlockSpec((mem_len, d), lambda b, i: (b, 0)),
                  pl.BlockSpec((mem_len, d), lambda b, i: (b, 1)),
                  pl.BlockSpec((tq, d), lambda b, i: (b * nq + i, gate_block))],
        out_specs=pl.BlockSpec((tq, d), lambda b, i: (b * nq + i, 0)),
        out_shape=jax.ShapeDtypeStruct((n, d), BF16),
        compiler_params=_cparams(("parallel", "arbitrary")),
        name="cross_attention",
    )(acts, mem_kv, mem_kv, acts)


def _merge_kernel(x_ref, ca_ref, da_ref, xa_ref, g0_ref, g1_ref, g2_ref, wc_ref, wd_ref, wx_ref, wo_ref, o_ref):
    y = g0_ref[...].astype(F32) * jnp.dot(ca_ref[...], wc_ref[...], preferred_element_type=F32)
    y = y + g1_ref[...].astype(F32) * jnp.dot(da_ref[...], wd_ref[...], preferred_element_type=F32)
    y = y + g2_ref[...].astype(F32) * jnp.dot(xa_ref[...], wx_ref[...], preferred_element_type=F32)
    o_ref[...] = x_ref[...] + jnp.dot(y.astype(BF16), wo_ref[...], preferred_element_type=F32)


def _merge_out(x2d, ca, da, xa, acts, gate_block0, wc, wd, wx, wo, tm):
    n, d = x2d.shape
    row = lambda i: (i, 0)
    fixed = lambda i: (0, 0)
    act_spec = pl.BlockSpec((tm, d), row)
    w_spec = pl.BlockSpec((d, d), fixed)
    gate_specs = [pl.BlockSpec((tm, d), functools.partial(lambda i, c: (i, c), c=gate_block0 + c))
                  for c in range(N_BRANCH)]
    return pl.pallas_call(
        _merge_kernel,
        grid=(n // tm,),
        in_specs=[pl.BlockSpec((tm, d), row), act_spec, act_spec, act_spec, *gate_specs,
                  w_spec, w_spec, w_spec, w_spec],
        out_specs=pl.BlockSpec((tm, d), row),
        out_shape=jax.ShapeDtypeStruct((n, d), x2d.dtype),
        compiler_params=_cparams(("parallel",)),
        name="merge_out",
    )(x2d, ca, da, xa, acts, acts, acts, wc, wd, wx, wo)


def _layer(x, mem, l, norm_g, mem_norm_g, w_in, conv_dw, conv_dw_b, conv_ln_g, conv_ln_b, w_conv_proj,
           diff_qn_g, diff_kn_g, lambda_q1, lambda_k1, lambda_q2, lambda_k2, diff_subln_g, w_diff_proj,
           w_mem_kv, x_qn_g, x_kn_g, w_x_proj, w_out):
    batch, seq, d = x.shape
    mem_len = mem.shape[1]
    heads = d // DIFF_V_DIM
    x_head_dim = d // X_HEADS
    assert x_head_dim == V7X_MXU_DIM and d % V7X_MXU_DIM == 0
    n = batch * seq
    x2d = x.reshape(n, d)
    lambda_init = 0.8 - 0.6 * math.exp(-0.3 * l)
    slopes = 2.0 ** (-8.0 * jnp.arange(1, heads + 1, dtype=F32) / heads)

    w_in_b = w_in.astype(BF16)
    g64 = _group_ones(DIFF_HEAD_DIM)
    g256 = _group_ones(V7X_MXU_DIM)

    hb = _rmsnorm(x2d, norm_g, tm=1024)
    mem_hb = _rmsnorm(mem.reshape(batch * mem_len, d), mem_norm_g, tm=1024)

    u = _glu_proj(hb, w_in_b, tm=1024, tn=d, ncol=d)
    ones = jnp.ones((d,), F32)
    gains = jnp.stack([
        ones,
        jnp.tile(diff_qn_g, d // DIFF_HEAD_DIM) * (DIFF_HEAD_DIM ** -0.5),
        jnp.tile(diff_kn_g, d // DIFF_HEAD_DIM),
        ones, ones,
        jnp.tile(x_qn_g, X_HEADS) * (x_head_dim ** -0.5),
        ones, ones, ones, ones]).reshape(10, 1, d)
    modes = {"silu": (0, 4, 6), "norm64": (1, 2), "none": (3,), "norm256": (5,), "sigmoid": (7, 8, 9)}
    acts = _seg_proj(hb, w_in_b, 2, gains, g64, g256, modes, tm=1024, tn=d)
    C_GATE, D_Q, D_K, D_V, D_GATE, X_Q, X_GATE, MERGE = 0, 1, 2, 3, 4, 5, 6, 7

    mem_gains = jnp.stack([jnp.tile(x_kn_g, X_HEADS), ones]).reshape(2, 1, d)
    mem_modes = {"silu": (), "norm64": (), "none": (1,), "norm256": (0,), "sigmoid": ()}
    mem_kv = _seg_proj(mem_hb, w_mem_kv.astype(BF16), 0, mem_gains, g64, g256, mem_modes, tm=1024, tn=d)

    conv_act = _conv_branch(u, acts, C_GATE, conv_dw, conv_dw_b, conv_ln_g, conv_ln_b, batch, seq, ts=256)
    lam_vecs = jnp.stack([lambda_q1, lambda_k1, lambda_q2, lambda_k2])
    diff_act = _diff_attention(acts, (D_Q, D_K, D_V, D_GATE), slopes, lam_vecs, diff_subln_g,
                               batch, seq, heads, tq=256, lambda_init=lambda_init)
    x_act = _cross_attention(acts, X_Q, X_GATE, mem_kv, batch, seq, mem_len, X_HEADS, tq=512)

    out = _merge_out(x2d, conv_act, diff_act, x_act, acts, MERGE,
                     w_conv_proj.astype(BF16), w_diff_proj.astype(BF16), w_x_proj.astype(BF16),
                     w_out.astype(BF16), tm=512)
    return out.reshape(batch, seq, d)


def kernel(x, mem, norm_g, mem_norm_g, w_in, conv_dw, conv_dw_b, conv_ln_g, conv_ln_b, w_conv_proj, diff_qn_g, diff_kn_g, lambda_q1, lambda_k1, lambda_q2, lambda_k2, diff_subln_g, w_diff_proj, w_mem_kv, x_qn_g, x_kn_g, w_x_proj, w_out):
    params = (norm_g, mem_norm_g, w_in, conv_dw, conv_dw_b, conv_ln_g, conv_ln_b, w_conv_proj, diff_qn_g,
              diff_kn_g, lambda_q1, lambda_k1, lambda_q2, lambda_k2, diff_subln_g, w_diff_proj, w_mem_kv,
              x_qn_g, x_kn_g, w_x_proj, w_out)
    for l in range(norm_g.shape[0]):
        x = _layer(x, mem, l, *(p[l] for p in params))
    return x
```

```python
import functools
import math

import jax
import jax.numpy as jnp
from jax import lax
from jax.experimental import pallas as pl
from jax.experimental.pallas import tpu as pltpu

CONV_K = 31
DIFF_HEAD_DIM = 64
DIFF_V_DIM = 2 * DIFF_HEAD_DIM
X_HEADS = 4
N_BRANCH = 3
RMS_EPS = 1e-6
LN_EPS = 1e-5
MASK_VALUE = -1e30
LOG2_E = math.log2(math.e)

V7X_LANES = 128
V7X_MXU_DIM = 256
V7X_VMEM_LIMIT_BYTES = 56 * 1024 * 1024

BF16 = jnp.bfloat16
F32 = jnp.float32


def _cparams(sem):
    return pltpu.CompilerParams(dimension_semantics=sem, vmem_limit_bytes=V7X_VMEM_LIMIT_BYTES)


def _sigmoid(x):
    return 1.0 / (1.0 + jnp.exp(-x))


def _silu(x):
    return x * _sigmoid(x)


def _nt_dot(a, b):
    return lax.dot_general(a, b, (((1,), (1,)), ((), ())), preferred_element_type=F32)


def _rmsnorm_kernel(x_ref, g_ref, o_ref):
    x = x_ref[...]
    y = x * lax.rsqrt(jnp.mean(x * x, axis=-1, keepdims=True) + RMS_EPS)
    o_ref[...] = (y * g_ref[...]).astype(o_ref.dtype)


def _rmsnorm(x2d, g, tm):
    n, d = x2d.shape
    return pl.pallas_call(
        _rmsnorm_kernel,
        grid=(n // tm,),
        in_specs=[pl.BlockSpec((tm, d), lambda i: (i, 0)),
                  pl.BlockSpec((1, d), lambda i: (0, 0))],
        out_specs=pl.BlockSpec((tm, d), lambda i: (i, 0)),
        out_shape=jax.ShapeDtypeStruct((n, d), BF16),
        compiler_params=_cparams(("parallel",)),
        name="rmsnorm",
    )(x2d, g.reshape(1, d))


def _group_rms(x, gmat, group, gain):
    x2 = x * x
    hi = x2.astype(BF16)
    lo = (x2 - hi.astype(F32)).astype(BF16)
    n = x.shape[1]
    parts = []
    for c in range(n // V7X_MXU_DIM):
        sl = slice(c * V7X_MXU_DIM, (c + 1) * V7X_MXU_DIM)
        parts.append(jnp.dot(hi[:, sl], gmat, preferred_element_type=F32)
                     + jnp.dot(lo[:, sl], gmat, preferred_element_type=F32))
    ss = jnp.concatenate(parts, axis=1)
    return x * lax.rsqrt(ss * (1.0 / group) + RMS_EPS) * gain


def _glu_kernel(h_ref, wa_ref, wb_ref, o_ref):
    h = h_ref[...]
    a = jnp.dot(h, wa_ref[...], preferred_element_type=F32)
    b = jnp.dot(h, wb_ref[...], preferred_element_type=F32)
    o_ref[...] = (a * _sigmoid(b)).astype(o_ref.dtype)


def _glu_proj(hb, w, tm, tn, ncol):
    n, d = hb.shape
    nb = ncol // tn
    return pl.pallas_call(
        _glu_kernel,
        grid=(n // tm, nb),
        in_specs=[pl.BlockSpec((tm, d), lambda i, j: (i, 0)),
                  pl.BlockSpec((d, tn), lambda i, j: (0, j)),
                  pl.BlockSpec((d, tn), lambda i, j: (0, nb + j))],
        out_specs=pl.BlockSpec((tm, tn), lambda i, j: (i, j)),
        out_shape=jax.ShapeDtypeStruct((n, ncol), BF16),
        compiler_params=_cparams(("parallel", "arbitrary")),
        name="glu_proj",
    )(hb, w, w)


def _seg_proj_kernel(h_ref, w_ref, gain_ref, g64_ref, g256_ref, o_ref, *, modes):
    j = pl.program_id(1)
    acc = jnp.dot(h_ref[...], w_ref[...], preferred_element_type=F32)

    def emit(fn, blocks):
        if not blocks:
            return
        cond = functools.reduce(jnp.logical_or, [j == b for b in blocks])

        @pl.when(cond)
        def _():
            o_ref[...] = fn(acc).astype(o_ref.dtype)

    emit(_silu, modes["silu"])
    emit(_sigmoid, modes["sigmoid"])
    emit(lambda a: a, modes["none"])
    emit(lambda a: _group_rms(a, g64_ref[...], DIFF_HEAD_DIM, gain_ref[...]), modes["norm64"])
    emit(lambda a: _group_rms(a, g256_ref[...], V7X_MXU_DIM, gain_ref[...]), modes["norm256"])


def _seg_proj(hb, w, col_block0, gains, g64, g256, modes, tm, tn):
    n, d = hb.shape
    nblk = gains.shape[0]
    return pl.pallas_call(
        functools.partial(_seg_proj_kernel, modes=modes),
        grid=(n // tm, nblk),
        in_specs=[pl.BlockSpec((tm, d), lambda i, j: (i, 0)),
                  pl.BlockSpec((d, tn), lambda i, j: (0, col_block0 + j)),
                  pl.BlockSpec((None, 1, tn), lambda i, j: (j, 0, 0)),
                  pl.BlockSpec((V7X_MXU_DIM, V7X_MXU_DIM), lambda i, j: (0, 0)),
                  pl.BlockSpec((V7X_MXU_DIM, V7X_MXU_DIM), lambda i, j: (0, 0))],
        out_specs=pl.BlockSpec((tm, tn), lambda i, j: (i, j)),
        out_shape=jax.ShapeDtypeStruct((n, nblk * tn), BF16),
        compiler_params=_cparams(("parallel", "arbitrary")),
        name="seg_proj",
    )(hb, w, gains, g64, g256)


def _group_ones(group):
    r = jnp.arange(V7X_MXU_DIM) // group
    return (r[:, None] == r[None, :]).astype(BF16)


CONV_HALO = 32
CONV_ROWS = 32
CONV_COLS = 512


def _conv_kernel(u_ref, gate_ref, dw_ref, dwb_ref, lng_ref, lnb_ref, o_ref, win_ref, y_ref, *, ts):
    j = pl.program_id(1)
    t0 = pl.multiple_of(j * ts, ts)
    d = u_ref.shape[1]

    @pl.when(j == 0)
    def _():
        win_ref[0:CONV_HALO, :] = jnp.zeros((CONV_HALO, d), F32)

    @pl.when(j > 0)
    def _():
        win_ref[0:CONV_HALO, :] = u_ref[pl.ds(t0 - CONV_HALO, CONV_HALO), :].astype(F32)

    win_ref[CONV_HALO:CONV_HALO + ts, :] = u_ref[pl.ds(t0, ts), :].astype(F32)

    first = CONV_HALO - (CONV_K - 1)

    def row_chunk(r, carry):
        r0 = pl.multiple_of(r * CONV_ROWS, CONV_ROWS)
        for c in range(d // CONV_COLS):
            cs = slice(c * CONV_COLS, (c + 1) * CONV_COLS)
            acc = jnp.zeros((CONV_ROWS, CONV_COLS), F32)
            w = win_ref[pl.ds(r0, CONV_ROWS + CONV_HALO), cs]
            for k in range(CONV_K):
                acc = acc + w[first + k:first + k + CONV_ROWS, :] * dw_ref[k:k + 1, cs]
            y_ref[pl.ds(r0, CONV_ROWS), cs] = acc + dwb_ref[:, cs]
        return carry

    lax.fori_loop(0, ts // CONV_ROWS, row_chunk, 0)

    y = y_ref[...]
    mu = jnp.mean(y, axis=-1, keepdims=True)
    yc = y - mu
    yn = yc * lax.rsqrt(jnp.mean(yc * yc, axis=-1, keepdims=True) + LN_EPS)
    yn = yn * lng_ref[...] + lnb_ref[...]
    o_ref[...] = (_silu(yn) * gate_ref[...].astype(F32)).astype(o_ref.dtype)


def _conv_branch(u, gates, gate_block, dw, dwb, lng, lnb, batch, seq, ts):
    n, d = u.shape
    nt = seq // ts
    return pl.pallas_call(
        functools.partial(_conv_kernel, ts=ts),
        grid=(batch, nt),
        in_specs=[pl.BlockSpec((seq, d), lambda b, j: (b, 0)),
                  pl.BlockSpec((ts, d), lambda b, j: (b * nt + j, gate_block)),
                  pl.BlockSpec((CONV_K, d), lambda b, j: (0, 0)),
                  pl.BlockSpec((1, d), lambda b, j: (0, 0)),
                  pl.BlockSpec((1, d), lambda b, j: (0, 0)),
                  pl.BlockSpec((1, d), lambda b, j: (0, 0))],
        out_specs=pl.BlockSpec((ts, d), lambda b, j: (b * nt + j, 0)),
        out_shape=jax.ShapeDtypeStruct((n, d), BF16),
        scratch_shapes=[pltpu.VMEM((CONV_HALO + ts, d), F32), pltpu.VMEM((ts, d), F32)],
        compiler_params=_cparams(("parallel", "arbitrary")),
        name="conv_branch",
    )(u, gates, dw, dwb.reshape(1, d), lng.reshape(1, d), lnb.reshape(1, d))


def _diff_attn_kernel(slopes_ref, q_ref, k_ref, v_ref, gate_ref, lam_ref, subg_ref, o_ref,
                      bias_ref, vt_ref, qt_ref, m_ref, l_ref, acc_ref, *, tq, lambda_init):
    h = pl.program_id(1)
    slope = slopes_ref[h]
    seq = q_ref.shape[0]
    nq = seq // tq

    keys = lax.broadcasted_iota(jnp.int32, (tq, tq), 0)
    queries = lax.broadcasted_iota(jnp.int32, (tq, tq), 1)
    rel = (keys - queries).astype(F32) * slope
    bias_ref[0] = rel
    bias_ref[1] = jnp.where(keys <= queries, rel, MASK_VALUE)

    for c in range(nq):
        vt_ref[c] = v_ref[c * tq:(c + 1) * tq, :].astype(F32).T.astype(vt_ref.dtype)

    sub = lax.broadcasted_iota(jnp.int32, (DIFF_V_DIM, tq), 0)
    first_map = sub < DIFF_HEAD_DIM

    lam_v = lam_ref[...]
    lam = (jnp.exp(jnp.sum(lam_v[0:1] * lam_v[1:2], axis=-1, keepdims=True))
           - jnp.exp(jnp.sum(lam_v[2:3] * lam_v[3:4], axis=-1, keepdims=True)) + lambda_init)

    def kv_block(jb, i, diag):
        s0 = pl.multiple_of(jb * tq, tq)
        k = k_ref[pl.ds(s0, tq), :]
        vt = vt_ref[jb]
        bias = bias_ref[1] if diag else bias_ref[0]
        shift = slope * ((jb - i) * tq).astype(F32)
        for mp in range(2):
            x = jnp.dot(k, qt_ref[mp], preferred_element_type=F32) + bias
            m_old = m_ref[mp]
            m_new = jnp.maximum(m_old, jnp.max(x, axis=0, keepdims=True) + shift)
            alpha = jnp.exp2(m_old - m_new)
            p = jnp.exp2(x - (m_new - shift))
            l_ref[mp] = alpha * l_ref[mp] + jnp.sum(p, axis=0, keepdims=True)
            acc_ref[mp] = alpha * acc_ref[mp] + jnp.dot(vt, p.astype(vt.dtype), preferred_element_type=F32)
            m_ref[mp] = m_new

    for i in range(nq):
        rows = slice(i * tq, (i + 1) * tq)
        qt = q_ref[rows, :].astype(F32).T
        qt_ref[0] = jnp.where(first_map, qt, 0.0).astype(qt_ref.dtype)
        qt_ref[1] = jnp.where(first_map, 0.0, qt).astype(qt_ref.dtype)
        m_ref[...] = jnp.full(m_ref.shape, MASK_VALUE, F32)
        l_ref[...] = jnp.zeros(l_ref.shape, F32)
        acc_ref[...] = jnp.zeros(acc_ref.shape, F32)

        def body(jb, carry, i=i):
            kv_block(jb, i, False)
            return carry

        lax.fori_loop(0, i, body, 0)
        kv_block(jnp.int32(i), i, True)

        o = acc_ref[0] / l_ref[0] - lam * (acc_ref[1] / l_ref[1])
        o = o * lax.rsqrt(jnp.mean(o * o, axis=0, keepdims=True) + RMS_EPS)
        o = o.T * (subg_ref[...] * (1.0 - lambda_init))
        o_ref[rows, :] = (o * gate_ref[rows, :].astype(F32)).astype(o_ref.dtype)


def _diff_attention(acts, blocks, slopes, lam_vecs, subg, batch, seq, heads, tq, lambda_init):
    n = acts.shape[0]
    nq = seq // tq
    qb, kb, vb, gb = (b * heads for b in blocks)
    kernel = functools.partial(_diff_attn_kernel, tq=tq, lambda_init=lambda_init)
    head_spec = lambda first: pl.BlockSpec((seq, DIFF_V_DIM), lambda b, h: (b, first + h))
    return pl.pallas_call(
        kernel,
        grid=(batch, heads),
        in_specs=[pl.BlockSpec(memory_space=pltpu.SMEM),
                  head_spec(qb), head_spec(kb), head_spec(vb), head_spec(gb),
                  pl.BlockSpec((4, DIFF_HEAD_DIM), lambda b, h: (0, 0)),
                  pl.BlockSpec((1, DIFF_V_DIM), lambda b, h: (0, 0))],
        out_specs=head_spec(0),
        out_shape=jax.ShapeDtypeStruct((n, heads * DIFF_V_DIM), BF16),
        scratch_shapes=[pltpu.VMEM((2, tq, tq), F32),
                        pltpu.VMEM((nq, DIFF_V_DIM, tq), BF16),
                        pltpu.VMEM((2, DIFF_V_DIM, tq), BF16),
                        pltpu.VMEM((2, 1, tq), F32),
                        pltpu.VMEM((2, 1, tq), F32),
                        pltpu.VMEM((2, DIFF_V_DIM, tq), F32)],
        compiler_params=_cparams(("parallel", "parallel")),
        name="diff_attention",
    )(slopes, acts, acts, acts, acts, lam_vecs, subg.reshape(1, DIFF_V_DIM))


def _xattn_kernel(q_ref, k_ref, v_ref, gate_ref, o_ref, *, heads):
    hd = q_ref.shape[1] // heads
    for h in range(heads):
        sl = slice(h * hd, (h + 1) * hd)
        s = _nt_dot(q_ref[:, sl], k_ref[:, sl])
        m = jnp.max(s, axis=-1, keepdims=True)
        p = jnp.exp(s - m)
        l = jnp.sum(p, axis=-1, keepdims=True)
        o = jnp.dot(p.astype(BF16), v_ref[:, sl], preferred_element_type=F32) / l
        o_ref[:, sl] = (o * gate_ref[:, sl].astype(F32)).astype(o_ref.dtype)


def _cross_attention(acts, q_block, gate_block, mem_kv, batch, seq, mem_len, heads, tq):
    n = acts.shape[0]
    d = mem_kv.shape[1] // 2
    nq = seq // tq
    return pl.pallas_call(
        functools.partial(_xattn_kernel, heads=heads),
        grid=(batch, nq),
        in_specs=[pl.BlockSpec((tq, d), lambda b, i: (b * nq + i, q_block)),
                  pl.BlockSpec((mem_len, d), lambda b, i: (b, 0)),
                  pl.BlockSpec((mem_len, d), lambda b, i: (b, 1)),
                  pl.BlockSpec((tq, d), lambda b, i: (b * nq + i, gate_block))],
        out_specs=pl.BlockSpec((tq, d), lambda b, i: (b * nq + i, 0)),
        out_shape=jax.ShapeDtypeStruct((n, d), BF16),
        compiler_params=_cparams(("parallel", "arbitrary")),
        name="cross_attention",
    )(acts, mem_kv, mem_kv, acts)


def _merge_kernel(x_ref, ca_ref, da_ref, xa_ref, g0_ref, g1_ref, g2_ref, wc_ref, wd_ref, wx_ref, wo_ref, o_ref):
    y = g0_ref[...].astype(F32) * jnp.dot(ca_ref[...], wc_ref[...], preferred_element_type=F32)
    y = y + g1_ref[...].astype(F32) * jnp.dot(da_ref[...], wd_ref[...], preferred_element_type=F32)
    y = y + g2_ref[...].astype(F32) * jnp.dot(xa_ref[...], wx_ref[...], preferred_element_type=F32)
    o_ref[...] = x_ref[...] + jnp.dot(y.astype(BF16), wo_ref[...], preferred_element_type=F32)


def _merge_out(x2d, ca, da, xa, acts, gate_block0, wc, wd, wx, wo, tm):
    n, d = x2d.shape
    row = lambda i: (i, 0)
    fixed = lambda i: (0, 0)
    act_spec = pl.BlockSpec((tm, d), row)
    w_spec = pl.BlockSpec((d, d), fixed)
    gate_specs = [pl.BlockSpec((tm, d), functools.partial(lambda i, c: (i, c), c=gate_block0 + c))
                  for c in range(N_BRANCH)]
    return pl.pallas_call(
        _merge_kernel,
        grid=(n // tm,),
        in_specs=[pl.BlockSpec((tm, d), row), act_spec, act_spec, act_spec, *gate_specs,
                  w_spec, w_spec, w_spec, w_spec],
        out_specs=pl.BlockSpec((tm, d), row),
        out_shape=jax.ShapeDtypeStruct((n, d), x2d.dtype),
        compiler_params=_cparams(("parallel",)),
        name="merge_out",
    )(x2d, ca, da, xa, acts, acts, acts, wc, wd, wx, wo)


def _layer(x, mem, l, norm_g, mem_norm_g, w_in, conv_dw, conv_dw_b, conv_ln_g, conv_ln_b, w_conv_proj,
           diff_qn_g, diff_kn_g, lambda_q1, lambda_k1, lambda_q2, lambda_k2, diff_subln_g, w_diff_proj,
           w_mem_kv, x_qn_g, x_kn_g, w_x_proj, w_out):
    batch, seq, d = x.shape
    mem_len = mem.shape[1]
    heads = d // DIFF_V_DIM
    x_head_dim = d // X_HEADS
    assert x_head_dim == V7X_MXU_DIM and d % V7X_MXU_DIM == 0
    n = batch * seq
    x2d = x.reshape(n, d)
    lambda_init = 0.8 - 0.6 * math.exp(-0.3 * l)
    slopes = 2.0 ** (-8.0 * jnp.arange(1, heads + 1, dtype=F32) / heads) * LOG2_E

    w_in_b = w_in.astype(BF16)
    g64 = _group_ones(DIFF_HEAD_DIM)
    g256 = _group_ones(V7X_MXU_DIM)

    hb = _rmsnorm(x2d, norm_g, tm=1024)
    mem_hb = _rmsnorm(mem.reshape(batch * mem_len, d), mem_norm_g, tm=1024)

    u = _glu_proj(hb, w_in_b, tm=1024, tn=d, ncol=d)
    ones = jnp.ones((d,), F32)
    gains = jnp.stack([
        ones,
        jnp.tile(diff_qn_g, d // DIFF_HEAD_DIM) * (DIFF_HEAD_DIM ** -0.5 * LOG2_E),
        jnp.tile(diff_kn_g, d // DIFF_HEAD_DIM),
        ones, ones,
        jnp.tile(x_qn_g, X_HEADS) * (x_head_dim ** -0.5),
        ones, ones, ones, ones]).reshape(10, 1, d)
    modes = {"silu": (0, 4, 6), "norm64": (1, 2), "none": (3,), "norm256": (5,), "sigmoid": (7, 8, 9)}
    acts = _seg_proj(hb, w_in_b, 2, gains, g64, g256, modes, tm=1024, tn=d)
    C_GATE, D_Q, D_K, D_V, D_GATE, X_Q, X_GATE, MERGE = 0, 1, 2, 3, 4, 5, 6, 7

    mem_gains = jnp.stack([jnp.tile(x_kn_g, X_HEADS), ones]).reshape(2, 1, d)
    mem_modes = {"silu": (), "norm64": (), "none": (1,), "norm256": (0,), "sigmoid": ()}
    mem_kv = _seg_proj(mem_hb, w_mem_kv.astype(BF16), 0, mem_gains, g64, g256, mem_modes, tm=1024, tn=d)

    conv_act = _conv_branch(u, acts, C_GATE, conv_dw, conv_dw_b, conv_ln_g, conv_ln_b, batch, seq, ts=256)
    lam_vecs = jnp.stack([lambda_q1, lambda_k1, lambda_q2, lambda_k2])
    diff_act = _diff_attention(acts, (D_Q, D_K, D_V, D_GATE), slopes, lam_vecs, diff_subln_g,
                               batch, seq, heads, tq=256, lambda_init=lambda_init)
    x_act = _cross_attention(acts, X_Q, X_GATE, mem_kv, batch, seq, mem_len, X_HEADS, tq=512)

    out = _merge_out(x2d, conv_act, diff_act, x_act, acts, MERGE,
                     w_conv_proj.astype(BF16), w_diff_proj.astype(BF16), w_x_proj.astype(BF16),
                     w_out.astype(BF16), tm=512)
    return out.reshape(batch, seq, d)


def kernel(x, mem, norm_g, mem_norm_g, w_in, conv_dw, conv_dw_b, conv_ln_g, conv_ln_b, w_conv_proj, diff_qn_g, diff_kn_g, lambda_q1, lambda_k1, lambda_q2, lambda_k2, diff_subln_g, w_diff_proj, w_mem_kv, x_qn_g, x_kn_g, w_x_proj, w_out):
    params = (norm_g, mem_norm_g, w_in, conv_dw, conv_dw_b, conv_ln_g, conv_ln_b, w_conv_proj, diff_qn_g,
              diff_kn_g, lambda_q1, lambda_k1, lambda_q2, lambda_k2, diff_subln_g, w_diff_proj, w_mem_kv,
              x_qn_g, x_kn_g, w_x_proj, w_out)
    for l in range(norm_g.shape[0]):
        x = _layer(x, mem, l, *(p[l] for p in params))
    return x
```

```python
import functools
import math

import jax
import jax.numpy as jnp
from jax import lax
from jax.experimental import pallas as pl
from jax.experimental.pallas import tpu as pltpu

CONV_K = 31
DIFF_HEAD_DIM = 64
DIFF_V_DIM = 2 * DIFF_HEAD_DIM
X_HEADS = 4
N_BRANCH = 3
RMS_EPS = 1e-6
LN_EPS = 1e-5
MASK_VALUE = -1e30
LOG2_E = math.log2(math.e)

V7X_LANES = 128
V7X_SUBLANES = 8
V7X_MXU_DIM = 256
V7X_VMEM_LIMIT_BYTES = 56 * 1024 * 1024

BF16 = jnp.bfloat16
F32 = jnp.float32


def _cparams(sem):
    return pltpu.CompilerParams(dimension_semantics=sem, vmem_limit_bytes=V7X_VMEM_LIMIT_BYTES)


def _sigmoid(x):
    return 1.0 / (1.0 + jnp.exp(-x))


def _silu(x):
    return x * _sigmoid(x)


def _nt_dot(a, b):
    return lax.dot_general(a, b, (((1,), (1,)), ((), ())), preferred_element_type=F32)


PROJ_TN = 1024


def _group_rms(x, gmat, group, gain):
    x2 = x * x
    hi = x2.astype(BF16)
    lo = (x2 - hi.astype(F32)).astype(BF16)
    n = x.shape[1]
    parts = []
    for c in range(n // V7X_MXU_DIM):
        sl = slice(c * V7X_MXU_DIM, (c + 1) * V7X_MXU_DIM)
        parts.append(jnp.dot(hi[:, sl], gmat, preferred_element_type=F32)
                     + jnp.dot(lo[:, sl], gmat, preferred_element_type=F32))
    ss = jnp.concatenate(parts, axis=1)
    return x * lax.rsqrt(ss * (1.0 / group) + RMS_EPS) * gain


def _norm_proj_kernel(x_ref, g_ref, w_ref, gain_ref, g64_ref, g256_ref, o_ref, *, ops):
    x = x_ref[...]
    h = (x * lax.rsqrt(jnp.mean(x * x, axis=-1, keepdims=True) + RMS_EPS) * g_ref[...]).astype(BF16)

    def proj(blk):
        return jnp.dot(h, w_ref[:, blk * PROJ_TN:(blk + 1) * PROJ_TN], preferred_element_type=F32)

    for ob, (kind, blks) in enumerate(ops):
        if kind == "glu":
            y = proj(blks[0]) * _sigmoid(proj(blks[1]))
        elif kind == "silu":
            y = _silu(proj(blks[0]))
        elif kind == "sigmoid":
            y = _sigmoid(proj(blks[0]))
        elif kind == "norm64":
            y = _group_rms(proj(blks[0]), g64_ref[...], DIFF_HEAD_DIM, gain_ref[ob])
        elif kind == "norm256":
            y = _group_rms(proj(blks[0]), g256_ref[...], V7X_MXU_DIM, gain_ref[ob])
        else:
            assert kind == "none", kind
            y = proj(blks[0])
        o_ref[:, ob * PROJ_TN:(ob + 1) * PROJ_TN] = y.astype(o_ref.dtype)


def _norm_proj(x2d, g, w, ops, gains, g64, g256, tm):
    n, d = x2d.shape
    nout = len(ops) * PROJ_TN

    def resident(a):
        return pl.BlockSpec(a.shape, lambda i: (0,) * a.ndim, pipeline_mode=pl.Buffered(1))

    g2 = g.reshape(1, d)
    return pl.pallas_call(
        functools.partial(_norm_proj_kernel, ops=ops),
        grid=(n // tm,),
        in_specs=[pl.BlockSpec((tm, d), lambda i: (i, 0)),
                  resident(g2), resident(w), resident(gains), resident(g64), resident(g256)],
        out_specs=pl.BlockSpec((tm, nout), lambda i: (i, 0)),
        out_shape=jax.ShapeDtypeStruct((n, nout), BF16),
        compiler_params=_cparams(("parallel",)),
        name="norm_proj",
    )(x2d, g2, w, gains, g64, g256)


def _group_ones(group):
    r = jnp.arange(V7X_MXU_DIM) // group
    return (r[:, None] == r[None, :]).astype(BF16)


CONV_HALO = 32
CONV_ROWS = 32
CONV_COLS = 512


def _conv_kernel(u_ref, gate_ref, dw_ref, dwb_ref, lng_ref, lnb_ref, o_ref, win_ref, y_ref, *, ts):
    j = pl.program_id(1)
    t0 = pl.multiple_of(j * ts, ts)
    d = u_ref.shape[1]

    @pl.when(j == 0)
    def _():
        win_ref[0, 0:CONV_HALO, :] = jnp.zeros((CONV_HALO, d), F32)

    @pl.when(j > 0)
    def _():
        win_ref[0, 0:CONV_HALO, :] = u_ref[pl.ds(t0 - CONV_HALO, CONV_HALO), :].astype(F32)

    win_ref[0, CONV_HALO:CONV_HALO + ts, :] = u_ref[pl.ds(t0, ts), :].astype(F32)

    first = CONV_HALO - (CONV_K - 1)
    span = CONV_HALO + ts - V7X_SUBLANES
    for c in range(d // V7X_LANES):
        cs = slice(c * V7X_LANES, (c + 1) * V7X_LANES)
        x = win_ref[0, :, cs]
        for s in range(1, V7X_SUBLANES):
            x = pltpu.roll(x, x.shape[0] - 1, 0)
            win_ref[s, 0:span, cs] = x[0:span]

    for r0 in range(0, ts, CONV_ROWS):
        for c in range(d // CONV_COLS):
            cs = slice(c * CONV_COLS, (c + 1) * CONV_COLS)
            acc = jnp.zeros((CONV_ROWS, CONV_COLS), F32)
            for k in range(CONV_K):
                q, s = divmod(first + k, V7X_SUBLANES)
                a0 = r0 + q * V7X_SUBLANES
                w = jnp.concatenate([dw_ref[k, :, cs]] * (CONV_ROWS // V7X_SUBLANES), axis=0)
                acc = acc + win_ref[s, a0:a0 + CONV_ROWS, cs] * w
            y_ref[r0:r0 + CONV_ROWS, cs] = acc + dwb_ref[:, cs]

    y = y_ref[...]
    mu = jnp.mean(y, axis=-1, keepdims=True)
    yc = y - mu
    yn = yc * lax.rsqrt(jnp.mean(yc * yc, axis=-1, keepdims=True) + LN_EPS)
    yn = yn * lng_ref[...] + lnb_ref[...]
    o_ref[...] = (_silu(yn) * gate_ref[...].astype(F32)).astype(o_ref.dtype)


def _conv_branch(acts, u_block, gate_block, dw, dwb, lng, lnb, batch, seq, ts):
    n, d = acts.shape[0], dw.shape[1]
    nt = seq // ts
    return pl.pallas_call(
        functools.partial(_conv_kernel, ts=ts),
        grid=(batch, nt),
        in_specs=[pl.BlockSpec((seq, d), lambda b, j: (b, u_block)),
                  pl.BlockSpec((ts, d), lambda b, j: (b * nt + j, gate_block)),
                  pl.BlockSpec((CONV_K, V7X_SUBLANES, d), lambda b, j: (0, 0, 0)),
                  pl.BlockSpec((1, d), lambda b, j: (0, 0)),
                  pl.BlockSpec((1, d), lambda b, j: (0, 0)),
                  pl.BlockSpec((1, d), lambda b, j: (0, 0))],
        out_specs=pl.BlockSpec((ts, d), lambda b, j: (b * nt + j, 0)),
        out_shape=jax.ShapeDtypeStruct((n, d), BF16),
        scratch_shapes=[pltpu.VMEM((V7X_SUBLANES, CONV_HALO + ts, d), F32), pltpu.VMEM((ts, d), F32)],
        compiler_params=_cparams(("parallel", "arbitrary")),
        name="conv_branch",
    )(acts, acts, jnp.broadcast_to(dw[:, None, :], (CONV_K, V7X_SUBLANES, d)),
      dwb.reshape(1, d), lng.reshape(1, d), lnb.reshape(1, d))


ALIBI_ROWS = 8
ALIBI_PIECES = 3
POS_RADIX = 128
V_SUM_ROWS = 16


def _alibi_table(heads):
    rest = 2.0 ** (-8.0 * jnp.arange(1, heads + 1, dtype=F32) / heads) * LOG2_E
    pieces = []
    for _ in range(ALIBI_PIECES):
        piece = rest.astype(BF16).astype(F32)
        pieces.append(piece)
        rest = rest - piece
    zero = jnp.zeros_like(rest)
    rows = [POS_RADIX * p for p in pieces] + pieces + [zero] * (ALIBI_ROWS - 2 * ALIBI_PIECES)
    return jnp.broadcast_to(jnp.stack(rows, axis=1)[:, :, None], (heads, ALIBI_ROWS, V7X_LANES))


def _diff_attn_kernel(alibi_ref, q_ref, k_ref, v_ref, gate_ref, lam_ref, subg_ref, o_ref,
                      km_ref, vt_ref, qt_ref, mask_ref, *, tq, lambda_init):
    seq = q_ref.shape[0]
    nq = seq // tq
    d = DIFF_HEAD_DIM
    lanes = lax.broadcasted_iota(jnp.int32, (tq, DIFF_V_DIM), 1)

    for c in range(nq):
        rows = slice(c * tq, (c + 1) * tq)
        kf = k_ref[rows, :].astype(F32)
        pos = lax.broadcasted_iota(jnp.int32, (tq, DIFF_V_DIM), 0) + c * tq
        hi = (pos // POS_RADIX).astype(F32)
        lo = (pos % POS_RADIX).astype(F32)
        extra = jnp.where(lanes < d + ALIBI_PIECES, hi, jnp.where(lanes < d + 2 * ALIBI_PIECES, lo, 0.0))
        km_ref[0, rows, :] = jnp.where(lanes < d, kf, extra).astype(km_ref.dtype)
        km_ref[1, rows, :] = jnp.where(lanes < d, pltpu.roll(kf, d, 1), extra).astype(km_ref.dtype)
        vt_ref[:DIFF_V_DIM, rows] = v_ref[rows, :].astype(F32).T.astype(vt_ref.dtype)
    ones_row = lax.broadcasted_iota(jnp.int32, (V_SUM_ROWS, seq), 0) == 0
    vt_ref[DIFF_V_DIM:, :] = jnp.where(ones_row, 1.0, 0.0).astype(vt_ref.dtype)

    alibi = jnp.concatenate([alibi_ref[...]] * (tq // V7X_LANES), axis=1)
    pad = jnp.zeros((DIFF_V_DIM - d - ALIBI_ROWS, tq), F32)
    for i in range(nq):
        qt = q_ref[i * tq:(i + 1) * tq, :].astype(F32).T
        qt_ref[i, 0] = jnp.concatenate([qt[:d], alibi, pad], axis=0).astype(qt_ref.dtype)
        qt_ref[i, 1] = jnp.concatenate([qt[d:], alibi, pad], axis=0).astype(qt_ref.dtype)

    kk = lax.broadcasted_iota(jnp.int32, (tq, tq), 0)
    qq = lax.broadcasted_iota(jnp.int32, (tq, tq), 1)
    mask_ref[...] = jnp.where(kk <= qq, 0.0, MASK_VALUE)

    lam_v = lam_ref[...]
    lam = (jnp.exp(jnp.sum(lam_v[0:1] * lam_v[1:2], axis=-1, keepdims=True))
           - jnp.exp(jnp.sum(lam_v[2:3] * lam_v[3:4], axis=-1, keepdims=True)) + lambda_init)

    def scores(i):
        keys = (i + 1) * tq
        return [jnp.dot(km_ref[mp, :keys, :], qt_ref[i, mp], preferred_element_type=F32) for mp in range(2)]

    x_next = scores(0)
    for i in range(nq):
        x_cur = x_next
        if i + 1 < nq:
            x_next = scores(i + 1)
        keys = (i + 1) * tq
        vt = vt_ref[:, :keys]
        acc = []
        for x in x_cur:
            x = jnp.concatenate([x[:keys - tq], x[keys - tq:] + mask_ref[...]], axis=0) if i else x + mask_ref[...]
            p = jnp.exp2(x - jnp.max(x, axis=0, keepdims=True))
            acc.append(jnp.dot(vt, p.astype(vt.dtype), preferred_element_type=F32))
        rows = slice(i * tq, (i + 1) * tq)
        num = [a[:DIFF_V_DIM] for a in acc]
        den = [a[DIFF_V_DIM:DIFF_V_DIM + 1] for a in acc]
        o = num[0] / den[0] - lam * (num[1] / den[1])
        o = o * lax.rsqrt(jnp.mean(o * o, axis=0, keepdims=True) + RMS_EPS)
        o = o.T * (subg_ref[...] * (1.0 - lambda_init))
        o_ref[rows, :] = (o * gate_ref[rows, :].astype(F32)).astype(o_ref.dtype)


def _diff_attention(acts, blocks, alibi, lam_vecs, subg, batch, seq, heads, tq, lambda_init):
    n = acts.shape[0]
    nq = seq // tq
    qb, kb, vb, gb = (b * heads for b in blocks)
    kernel = functools.partial(_diff_attn_kernel, tq=tq, lambda_init=lambda_init)
    head_spec = lambda first: pl.BlockSpec((seq, DIFF_V_DIM), lambda b, h: (b, first + h))
    return pl.pallas_call(
        kernel,
        grid=(batch, heads),
        in_specs=[pl.BlockSpec((None, ALIBI_ROWS, V7X_LANES), lambda b, h: (h, 0, 0)),
                  head_spec(qb), head_spec(kb), head_spec(vb), head_spec(gb),
                  pl.BlockSpec((4, DIFF_HEAD_DIM), lambda b, h: (0, 0)),
                  pl.BlockSpec((1, DIFF_V_DIM), lambda b, h: (0, 0))],
        out_specs=head_spec(0),
        out_shape=jax.ShapeDtypeStruct((n, heads * DIFF_V_DIM), BF16),
        scratch_shapes=[pltpu.VMEM((2, seq, DIFF_V_DIM), BF16),
                        pltpu.VMEM((DIFF_V_DIM + V_SUM_ROWS, seq), BF16),
                        pltpu.VMEM((nq, 2, DIFF_V_DIM, tq), BF16),
                        pltpu.VMEM((tq, tq), F32)],
        compiler_params=_cparams(("parallel", "parallel")),
        name="diff_attention",
    )(alibi, acts, acts, acts, acts, lam_vecs, subg.reshape(1, DIFF_V_DIM))


def _xattn_kernel(q_ref, k_ref, v_ref, gate_ref, o_ref, *, heads):
    hd = q_ref.shape[1] // heads
    for h in range(heads):
        sl = slice(h * hd, (h + 1) * hd)
        s = _nt_dot(q_ref[:, sl], k_ref[:, sl])
        m = jnp.max(s, axis=-1, keepdims=True)
        p = jnp.exp(s - m)
        l = jnp.sum(p, axis=-1, keepdims=True)
        o = jnp.dot(p.astype(BF16), v_ref[:, sl], preferred_element_type=F32) / l
        o_ref[:, sl] = (o * gate_ref[:, sl].astype(F32)).astype(o_ref.dtype)


def _cross_attention(acts, q_block, gate_block, mem_kv, batch, seq, mem_len, heads, tq):
    n = acts.shape[0]
    d = mem_kv.shape[1] // 2
    nq = seq // tq
    return pl.pallas_call(
        functools.partial(_xattn_kernel, heads=heads),
        grid=(batch, nq),
        in_specs=[pl.BlockSpec((tq, d), lambda b, i: (b * nq + i, q_block)),
                  pl.BlockSpec((mem_len, d), lambda b, i: (b, 0)),
                  pl.BlockSpec((mem_len, d), lambda b, i: (b, 1)),
                  pl.BlockSpec((tq, d), lambda b, i: (b * nq + i, gate_block))],
        out_specs=pl.BlockSpec((tq, d), lambda b, i: (b * nq + i, 0)),
        out_shape=jax.ShapeDtypeStruct((n, d), BF16),
        compiler_params=_cparams(("parallel", "arbitrary")),
        name="cross_attention",
    )(acts, mem_kv, mem_kv, acts)


def _merge_kernel(x_ref, ca_ref, da_ref, xa_ref, g0_ref, g1_ref, g2_ref, wc_ref, wd_ref, wx_ref, wo_ref, o_ref):
    y = g0_ref[...].astype(F32) * jnp.dot(ca_ref[...], wc_ref[...], preferred_element_type=F32)
    y = y + g1_ref[...].astype(F32) * jnp.dot(da_ref[...], wd_ref[...], preferred_element_type=F32)
    y = y + g2_ref[...].astype(F32) * jnp.dot(xa_ref[...], wx_ref[...], preferred_element_type=F32)
    o_ref[...] = x_ref[...] + jnp.dot(y.astype(BF16), wo_ref[...], preferred_element_type=F32)


def _merge_out(x2d, ca, da, xa, acts, gate_block0, wc, wd, wx, wo, tm):
    n, d = x2d.shape
    row = lambda i: (i, 0)
    fixed = lambda i: (0, 0)
    act_spec = pl.BlockSpec((tm, d), row)
    w_spec = pl.BlockSpec((d, d), fixed)
    gate_specs = [pl.BlockSpec((tm, d), functools.partial(lambda i, c: (i, c), c=gate_block0 + c))
                  for c in range(N_BRANCH)]
    return pl.pallas_call(
        _merge_kernel,
        grid=(n // tm,),
        in_specs=[pl.BlockSpec((tm, d), row), act_spec, act_spec, act_spec, *gate_specs,
                  w_spec, w_spec, w_spec, w_spec],
        out_specs=pl.BlockSpec((tm, d), row),
        out_shape=jax.ShapeDtypeStruct((n, d), x2d.dtype),
        compiler_params=_cparams(("parallel",)),
        name="merge_out",
    )(x2d, ca, da, xa, acts, acts, acts, wc, wd, wx, wo)


def _layer(x, mem, l, norm_g, mem_norm_g, w_in, conv_dw, conv_dw_b, conv_ln_g, conv_ln_b, w_conv_proj,
           diff_qn_g, diff_kn_g, lambda_q1, lambda_k1, lambda_q2, lambda_k2, diff_subln_g, w_diff_proj,
           w_mem_kv, x_qn_g, x_kn_g, w_x_proj, w_out):
    batch, seq, d = x.shape
    mem_len = mem.shape[1]
    heads = d // DIFF_V_DIM
    x_head_dim = d // X_HEADS
    assert x_head_dim == V7X_MXU_DIM and d % V7X_MXU_DIM == 0
    n = batch * seq
    x2d = x.reshape(n, d)
    lambda_init = 0.8 - 0.6 * math.exp(-0.3 * l)
    alibi = _alibi_table(heads)

    assert d == PROJ_TN
    g64 = _group_ones(DIFF_HEAD_DIM)
    g256 = _group_ones(V7X_MXU_DIM)
    ones = jnp.ones((d,), F32)

    ops = (("glu", (0, 1)), ("silu", (2,)), ("norm64", (3,)), ("norm64", (4,)), ("none", (5,)), ("silu", (6,)),
           ("norm256", (7,)), ("silu", (8,)), ("sigmoid", (9,)), ("sigmoid", (10,)), ("sigmoid", (11,)))
    U, C_GATE, D_Q, D_K, D_V, D_GATE, X_Q, X_GATE, MERGE = range(9)
    gain_rows = {D_Q: jnp.tile(diff_qn_g, d // DIFF_HEAD_DIM) * (DIFF_HEAD_DIM ** -0.5 * LOG2_E),
                 D_K: jnp.tile(diff_kn_g, d // DIFF_HEAD_DIM),
                 X_Q: jnp.tile(x_qn_g, X_HEADS) * (x_head_dim ** -0.5)}
    gains = jnp.stack([gain_rows.get(b, ones) for b in range(len(ops))]).reshape(len(ops), 1, d)
    acts = _norm_proj(x2d, norm_g, w_in.astype(BF16), ops, gains, g64, g256, tm=256)

    mem_ops = (("norm256", (0,)), ("none", (1,)))
    mem_gains = jnp.stack([jnp.tile(x_kn_g, X_HEADS), ones]).reshape(2, 1, d)
    mem_kv = _norm_proj(mem.reshape(batch * mem_len, d), mem_norm_g, w_mem_kv.astype(BF16), mem_ops, mem_gains,
                        g64, g256, tm=256)

    conv_act = _conv_branch(acts, U, C_GATE, conv_dw, conv_dw_b, conv_ln_g, conv_ln_b, batch, seq, ts=256)
    lam_vecs = jnp.stack([lambda_q1, lambda_k1, lambda_q2, lambda_k2])
    diff_act = _diff_attention(acts, (D_Q, D_K, D_V, D_GATE), alibi, lam_vecs, diff_subln_g,
                               batch, seq, heads, tq=256, lambda_init=lambda_init)
    x_act = _cross_attention(acts, X_Q, X_GATE, mem_kv, batch, seq, mem_len, X_HEADS, tq=512)

    out = _merge_out(x2d, conv_act, diff_act, x_act, acts, MERGE,
                     w_conv_proj.astype(BF16), w_diff_proj.astype(BF16), w_x_proj.astype(BF16),
                     w_out.astype(BF16), tm=512)
    return out.reshape(batch, seq, d)


def kernel(x, mem, norm_g, mem_norm_g, w_in, conv_dw, conv_dw_b, conv_ln_g, conv_ln_b, w_conv_proj, diff_qn_g, diff_kn_g, lambda_q1, lambda_k1, lambda_q2, lambda_k2, diff_subln_g, w_diff_proj, w_mem_kv, x_qn_g, x_kn_g, w_x_proj, w_out):
    params = (norm_g, mem_norm_g, w_in, conv_dw, conv_dw_b, conv_ln_g, conv_ln_b, w_conv_proj, diff_qn_g,
              diff_kn_g, lambda_q1, lambda_k1, lambda_q2, lambda_k2, diff_subln_g, w_diff_proj, w_mem_kv,
              x_qn_g, x_kn_g, w_x_proj, w_out)
    for l in range(norm_g.shape[0]):
        x = _layer(x, mem, l, *(p[l] for p in params))
    return x
```

```python
import functools
import math

import jax
import jax.numpy as jnp
from jax import lax
from jax.experimental import pallas as pl
from jax.experimental.pallas import tpu as pltpu

CONV_K = 31
DIFF_HEAD_DIM = 64
DIFF_V_DIM = 2 * DIFF_HEAD_DIM
X_HEADS = 4
N_BRANCH = 3
RMS_EPS = 1e-6
LN_EPS = 1e-5
MASK_VALUE = -1e30
LOG2_E = math.log2(math.e)

V7X_LANES = 128
V7X_SUBLANES = 8
V7X_MXU_DIM = 256
V7X_VMEM_LIMIT_BYTES = 56 * 1024 * 1024

BF16 = jnp.bfloat16
F32 = jnp.float32


def _cparams(sem):
    return pltpu.CompilerParams(dimension_semantics=sem, vmem_limit_bytes=V7X_VMEM_LIMIT_BYTES)


def _sigmoid(x):
    return 1.0 / (1.0 + jnp.exp(-x))


def _silu(x):
    return x * _sigmoid(x)


def _nt_dot(a, b):
    return lax.dot_general(a, b, (((1,), (1,)), ((), ())), preferred_element_type=F32)


PROJ_TN = 1024


def _group_rms(x, gmat, group, gain):
    x2 = (x * x).astype(BF16)
    n = x.shape[1]
    ss = jnp.concatenate([jnp.dot(x2[:, c:c + V7X_MXU_DIM], gmat, preferred_element_type=F32)
                          for c in range(0, n, V7X_MXU_DIM)], axis=1)
    return x * lax.rsqrt(ss * (1.0 / group) + RMS_EPS) * gain


def _norm_proj_kernel(x_ref, g_ref, w_ref, gain_ref, g64_ref, g256_ref, o_ref, *, ops):
    x = x_ref[...]
    h = (x * lax.rsqrt(jnp.mean(x * x, axis=-1, keepdims=True) + RMS_EPS) * g_ref[...]).astype(BF16)

    def proj(blk):
        return jnp.dot(h, w_ref[:, blk * PROJ_TN:(blk + 1) * PROJ_TN], preferred_element_type=F32)

    for ob, (kind, blks) in enumerate(ops):
        if kind == "glu":
            y = proj(blks[0]) * _sigmoid(proj(blks[1]))
        elif kind == "silu":
            y = _silu(proj(blks[0]))
        elif kind == "sigmoid":
            y = _sigmoid(proj(blks[0]))
        elif kind == "norm64":
            y = _group_rms(proj(blks[0]), g64_ref[...], DIFF_HEAD_DIM, gain_ref[ob])
        elif kind == "norm256":
            y = _group_rms(proj(blks[0]), g256_ref[...], V7X_MXU_DIM, gain_ref[ob])
        else:
            assert kind == "none", kind
            y = proj(blks[0])
        o_ref[:, ob * PROJ_TN:(ob + 1) * PROJ_TN] = y.astype(o_ref.dtype)


def _norm_proj(x2d, g, w, ops, gains, g64, g256, tm):
    n, d = x2d.shape
    nout = len(ops) * PROJ_TN

    def resident(a):
        return pl.BlockSpec(a.shape, lambda i: (0,) * a.ndim, pipeline_mode=pl.Buffered(1))

    g2 = g.reshape(1, d)
    return pl.pallas_call(
        functools.partial(_norm_proj_kernel, ops=ops),
        grid=(n // tm,),
        in_specs=[pl.BlockSpec((tm, d), lambda i: (i, 0)),
                  resident(g2), resident(w), resident(gains), resident(g64), resident(g256)],
        out_specs=pl.BlockSpec((tm, nout), lambda i: (i, 0)),
        out_shape=jax.ShapeDtypeStruct((n, nout), BF16),
        compiler_params=_cparams(("parallel",)),
        name="norm_proj",
    )(x2d, g2, w, gains, g64, g256)


def _group_ones(group):
    r = jnp.arange(V7X_MXU_DIM) // group
    return (r[:, None] == r[None, :]).astype(BF16)


CONV_HALO = 32
CONV_ROWS = 32
CONV_COLS = 512


def _conv_kernel(u_ref, gate_ref, dw_ref, dwb_ref, lng_ref, lnb_ref, o_ref, win_ref, y_ref, *, ts):
    j = pl.program_id(1)
    t0 = pl.multiple_of(j * ts, ts)
    d = u_ref.shape[1]

    @pl.when(j == 0)
    def _():
        win_ref[0, 0:CONV_HALO, :] = jnp.zeros((CONV_HALO, d), F32)

    @pl.when(j > 0)
    def _():
        win_ref[0, 0:CONV_HALO, :] = u_ref[pl.ds(t0 - CONV_HALO, CONV_HALO), :].astype(F32)

    win_ref[0, CONV_HALO:CONV_HALO + ts, :] = u_ref[pl.ds(t0, ts), :].astype(F32)

    first = CONV_HALO - (CONV_K - 1)
    span = CONV_HALO + ts - V7X_SUBLANES
    for c in range(d // V7X_LANES):
        cs = slice(c * V7X_LANES, (c + 1) * V7X_LANES)
        x = win_ref[0, :, cs]
        for s in range(1, V7X_SUBLANES):
            x = pltpu.roll(x, x.shape[0] - 1, 0)
            win_ref[s, 0:span, cs] = x[0:span]

    for r0 in range(0, ts, CONV_ROWS):
        for c in range(d // CONV_COLS):
            cs = slice(c * CONV_COLS, (c + 1) * CONV_COLS)
            acc = jnp.zeros((CONV_ROWS, CONV_COLS), F32)
            for s in range(V7X_SUBLANES):
                taps = [k for k in range(CONV_K) if (first + k) % V7X_SUBLANES == s]
                lo = r0 + first + taps[0] - s
                big = win_ref[s, lo:lo + (taps[-1] - taps[0]) + CONV_ROWS, cs]
                for k in taps:
                    w = jnp.concatenate([dw_ref[k, :, cs]] * (CONV_ROWS // V7X_SUBLANES), axis=0)
                    acc = acc + big[k - taps[0]:k - taps[0] + CONV_ROWS] * w
            y_ref[r0:r0 + CONV_ROWS, cs] = acc + dwb_ref[:, cs]

    y = y_ref[...]
    mu = jnp.mean(y, axis=-1, keepdims=True)
    yc = y - mu
    yn = yc * lax.rsqrt(jnp.mean(yc * yc, axis=-1, keepdims=True) + LN_EPS)
    yn = yn * lng_ref[...] + lnb_ref[...]
    o_ref[...] = (_silu(yn) * gate_ref[...].astype(F32)).astype(o_ref.dtype)


def _conv_branch(acts, u_block, gate_block, dw, dwb, lng, lnb, batch, seq, ts):
    n, d = acts.shape[0], dw.shape[1]
    nt = seq // ts
    return pl.pallas_call(
        functools.partial(_conv_kernel, ts=ts),
        grid=(batch, nt),
        in_specs=[pl.BlockSpec((seq, d), lambda b, j: (b, u_block)),
                  pl.BlockSpec((ts, d), lambda b, j: (b * nt + j, gate_block)),
                  pl.BlockSpec((CONV_K, V7X_SUBLANES, d), lambda b, j: (0, 0, 0)),
                  pl.BlockSpec((1, d), lambda b, j: (0, 0)),
                  pl.BlockSpec((1, d), lambda b, j: (0, 0)),
                  pl.BlockSpec((1, d), lambda b, j: (0, 0))],
        out_specs=pl.BlockSpec((ts, d), lambda b, j: (b * nt + j, 0)),
        out_shape=jax.ShapeDtypeStruct((n, d), BF16),
        scratch_shapes=[pltpu.VMEM((V7X_SUBLANES, CONV_HALO + ts, d), F32), pltpu.VMEM((ts, d), F32)],
        compiler_params=_cparams(("parallel", "arbitrary")),
        name="conv_branch",
    )(acts, acts, jnp.broadcast_to(dw[:, None, :], (CONV_K, V7X_SUBLANES, d)),
      dwb.reshape(1, d), lng.reshape(1, d), lnb.reshape(1, d))


EXTRA_ROWS = 16
PIECES = 3
POS_RADIX = 128
BF16_NORM_MARGIN = 1.02
MAX_FAST_BOUND = 40.0


def _pieces(v):
    out = []
    for _ in range(PIECES):
        piece = v.astype(BF16).astype(F32)
        out.append(piece)
        v = v - piece
    return out


def _attn_tables(heads, seq, qn_g, kn_g):
    d = DIFF_HEAD_DIM
    slopes = _pieces(2.0 ** (-8.0 * jnp.arange(1, heads + 1, dtype=F32) / heads) * LOG2_E)
    bound = (d ** 0.5 * LOG2_E * BF16_NORM_MARGIN) * jnp.max(jnp.abs(qn_g)) * jnp.max(jnp.abs(kn_g))
    fast = bound < MAX_FAST_BOUND
    neg_b = _pieces(jnp.broadcast_to(jnp.where(fast, -bound, 0.0), (heads,)))
    pos = jnp.arange(seq)
    hi = jnp.broadcast_to((pos // POS_RADIX).astype(F32), (heads, seq))
    lo = jnp.broadcast_to((pos % POS_RADIX).astype(F32), (heads, seq))
    const = lambda v: jnp.broadcast_to(v[:, None], (heads, seq))
    zero = jnp.zeros((heads, seq), F32)
    q_rows = ([const(POS_RADIX * p) for p in slopes] + [const(p) for p in slopes] + [hi] * PIECES + [lo] * PIECES
              + [const(p) for p in neg_b] + [zero])
    k_rows = ([hi] * PIECES + [lo] * PIECES + [const(-POS_RADIX * p) for p in slopes] + [const(-p) for p in slopes]
              + [zero + 1.0] * PIECES + [zero])
    q_extra = jnp.stack(q_rows, axis=1).astype(BF16)
    k_extra = jnp.stack(k_rows, axis=2)
    k_tab = jnp.zeros((heads, 2, seq, DIFF_V_DIM), F32)
    k_tab = k_tab.at[:, 0, :, d:d + EXTRA_ROWS].set(k_extra).at[:, 1, :, :EXTRA_ROWS].set(k_extra).astype(BF16)
    return q_extra, k_tab, fast.astype(jnp.int32).reshape(1)


def _diff_attn_kernel(fast_ref, qx_ref, kx_ref, q_ref, k_ref, v_ref, gate_ref, lam_ref, subg_ref, o_ref,
                      km_ref, vt_ref, qt_ref, mask_ref, *, tq, lambda_init):
    seq = q_ref.shape[0]
    nq = seq // tq
    d = DIFF_HEAD_DIM
    first_half = lax.broadcasted_iota(jnp.int32, (tq, DIFF_V_DIM), 1) < d

    for c in range(nq):
        rows = slice(c * tq, (c + 1) * tq)
        k = k_ref[rows, :]
        km_ref[0, rows, :] = jnp.where(first_half, k, kx_ref[0, rows, :])
        km_ref[1, rows, :] = jnp.where(first_half, kx_ref[1, rows, :], k)
        vt_ref[:, rows] = v_ref[rows, :].astype(F32).T.astype(vt_ref.dtype)

    pad = jnp.zeros((DIFF_V_DIM - d - EXTRA_ROWS, tq), F32)
    for i in range(nq):
        cols = slice(i * tq, (i + 1) * tq)
        extra = qx_ref[:, cols].astype(F32)
        qt = q_ref[cols, :].astype(F32).T
        qt_ref[i, 0] = jnp.concatenate([qt[:d], extra, pad], axis=0).astype(qt_ref.dtype)
        qt_ref[i, 1] = jnp.concatenate([extra, pad, qt[d:]], axis=0).astype(qt_ref.dtype)

    kk = lax.broadcasted_iota(jnp.int32, (tq, tq), 0)
    qq = lax.broadcasted_iota(jnp.int32, (tq, tq), 1)
    mask_ref[...] = jnp.where(kk <= qq, 0.0, MASK_VALUE)

    lam_v = lam_ref[...]
    lam = (jnp.exp(jnp.sum(lam_v[0:1] * lam_v[1:2], axis=-1, keepdims=True))
           - jnp.exp(jnp.sum(lam_v[2:3] * lam_v[3:4], axis=-1, keepdims=True)) + lambda_init)

    def scores(i):
        keys = (i + 1) * tq
        return [jnp.dot(km_ref[mp, :keys, :], qt_ref[i, mp], preferred_element_type=F32) for mp in range(2)]

    def attend(bounded):
        x_next = scores(0)
        for i in range(nq):
            x_cur = x_next
            if i + 1 < nq:
                x_next = scores(i + 1)
            keys = (i + 1) * tq
            vt = vt_ref[:, :keys]
            heads_out = []
            for x in x_cur:
                x = (jnp.concatenate([x[:keys - tq], x[keys - tq:] + mask_ref[...]], axis=0) if i
                     else x + mask_ref[...])
                p = jnp.exp2(x) if bounded else jnp.exp2(x - jnp.max(x, axis=0, keepdims=True))
                pv = jnp.dot(vt, p.astype(vt.dtype), preferred_element_type=F32)
                heads_out.append(pv / jnp.sum(p, axis=0, keepdims=True))
            rows = slice(i * tq, (i + 1) * tq)
            o = heads_out[0] - lam * heads_out[1]
            o = o * lax.rsqrt(jnp.mean(o * o, axis=0, keepdims=True) + RMS_EPS)
            o = o.T * (subg_ref[...] * (1.0 - lambda_init))
            o_ref[rows, :] = (o * gate_ref[rows, :].astype(F32)).astype(o_ref.dtype)

    fast = fast_ref[0] == 1
    pl.when(fast)(functools.partial(attend, True))
    pl.when(jnp.logical_not(fast))(functools.partial(attend, False))


def _diff_attention(acts, blocks, tables, lam_vecs, subg, batch, seq, heads, tq, lambda_init):
    n = acts.shape[0]
    nq = seq // tq
    q_extra, k_tab, fast = tables
    qb, kb, vb, gb = (b * heads for b in blocks)
    kernel = functools.partial(_diff_attn_kernel, tq=tq, lambda_init=lambda_init)
    head_spec = lambda first: pl.BlockSpec((seq, DIFF_V_DIM), lambda b, h: (b, first + h))
    return pl.pallas_call(
        kernel,
        grid=(batch, heads),
        in_specs=[pl.BlockSpec(memory_space=pltpu.SMEM),
                  pl.BlockSpec((None, EXTRA_ROWS, seq), lambda b, h: (h, 0, 0)),
                  pl.BlockSpec((None, 2, seq, DIFF_V_DIM), lambda b, h: (h, 0, 0, 0)),
                  head_spec(qb), head_spec(kb), head_spec(vb), head_spec(gb),
                  pl.BlockSpec((4, DIFF_HEAD_DIM), lambda b, h: (0, 0)),
                  pl.BlockSpec((1, DIFF_V_DIM), lambda b, h: (0, 0))],
        out_specs=head_spec(0),
        out_shape=jax.ShapeDtypeStruct((n, heads * DIFF_V_DIM), BF16),
        scratch_shapes=[pltpu.VMEM((2, seq, DIFF_V_DIM), BF16),
                        pltpu.VMEM((DIFF_V_DIM, seq), BF16),
                        pltpu.VMEM((nq, 2, DIFF_V_DIM, tq), BF16),
                        pltpu.VMEM((tq, tq), F32)],
        compiler_params=_cparams(("parallel", "parallel")),
        name="diff_attention",
    )(fast, q_extra, k_tab, acts, acts, acts, acts, lam_vecs, subg.reshape(1, DIFF_V_DIM))


def _xattn_kernel(q_ref, k_ref, v_ref, gate_ref, o_ref, *, heads):
    hd = q_ref.shape[1] // heads
    for h in range(heads):
        sl = slice(h * hd, (h + 1) * hd)
        s = _nt_dot(q_ref[:, sl], k_ref[:, sl])
        m = jnp.max(s, axis=-1, keepdims=True)
        p = jnp.exp(s - m)
        l = jnp.sum(p, axis=-1, keepdims=True)
        o = jnp.dot(p.astype(BF16), v_ref[:, sl], preferred_element_type=F32) / l
        o_ref[:, sl] = (o * gate_ref[:, sl].astype(F32)).astype(o_ref.dtype)


def _cross_attention(acts, q_block, gate_block, mem_kv, batch, seq, mem_len, heads, tq):
    n = acts.shape[0]
    d = mem_kv.shape[1] // 2
    nq = seq // tq
    return pl.pallas_call(
        functools.partial(_xattn_kernel, heads=heads),
        grid=(batch, nq),
        in_specs=[pl.BlockSpec((tq, d), lambda b, i: (b * nq + i, q_block)),
                  pl.BlockSpec((mem_len, d), lambda b, i: (b, 0)),
                  pl.BlockSpec((mem_len, d), lambda b, i: (b, 1)),
                  pl.BlockSpec((tq, d), lambda b, i: (b * nq + i, gate_block))],
        out_specs=pl.BlockSpec((tq, d), lambda b, i: (b * nq + i, 0)),
        out_shape=jax.ShapeDtypeStruct((n, d), BF16),
        compiler_params=_cparams(("parallel", "arbitrary")),
        name="cross_attention",
    )(acts, mem_kv, mem_kv, acts)


def _merge_kernel(x_ref, ca_ref, da_ref, xa_ref, g0_ref, g1_ref, g2_ref, wc_ref, wd_ref, wx_ref, wo_ref, o_ref):
    y = g0_ref[...].astype(F32) * jnp.dot(ca_ref[...], wc_ref[...], preferred_element_type=F32)
    y = y + g1_ref[...].astype(F32) * jnp.dot(da_ref[...], wd_ref[...], preferred_element_type=F32)
    y = y + g2_ref[...].astype(F32) * jnp.dot(xa_ref[...], wx_ref[...], preferred_element_type=F32)
    o_ref[...] = x_ref[...] + jnp.dot(y.astype(BF16), wo_ref[...], preferred_element_type=F32)


def _merge_out(x2d, ca, da, xa, acts, gate_block0, wc, wd, wx, wo, tm):
    n, d = x2d.shape
    row = lambda i: (i, 0)
    fixed = lambda i: (0, 0)
    act_spec = pl.BlockSpec((tm, d), row)
    w_spec = pl.BlockSpec((d, d), fixed)
    gate_specs = [pl.BlockSpec((tm, d), functools.partial(lambda i, c: (i, c), c=gate_block0 + c))
                  for c in range(N_BRANCH)]
    return pl.pallas_call(
        _merge_kernel,
        grid=(n // tm,),
        in_specs=[pl.BlockSpec((tm, d), row), act_spec, act_spec, act_spec, *gate_specs,
                  w_spec, w_spec, w_spec, w_spec],
        out_specs=pl.BlockSpec((tm, d), row),
        out_shape=jax.ShapeDtypeStruct((n, d), x2d.dtype),
        compiler_params=_cparams(("parallel",)),
        name="merge_out",
    )(x2d, ca, da, xa, acts, acts, acts, wc, wd, wx, wo)


def _layer(x, mem, l, norm_g, mem_norm_g, w_in, conv_dw, conv_dw_b, conv_ln_g, conv_ln_b, w_conv_proj,
           diff_qn_g, diff_kn_g, lambda_q1, lambda_k1, lambda_q2, lambda_k2, diff_subln_g, w_diff_proj,
           w_mem_kv, x_qn_g, x_kn_g, w_x_proj, w_out):
    batch, seq, d = x.shape
    mem_len = mem.shape[1]
    heads = d // DIFF_V_DIM
    x_head_dim = d // X_HEADS
    assert x_head_dim == V7X_MXU_DIM and d % V7X_MXU_DIM == 0
    n = batch * seq
    x2d = x.reshape(n, d)
    lambda_init = 0.8 - 0.6 * math.exp(-0.3 * l)
    attn_tables = _attn_tables(heads, seq, diff_qn_g, diff_kn_g)

    assert d == PROJ_TN
    g64 = _group_ones(DIFF_HEAD_DIM)
    g256 = _group_ones(V7X_MXU_DIM)
    ones = jnp.ones((d,), F32)

    ops = (("glu", (0, 1)), ("silu", (2,)), ("norm64", (3,)), ("norm64", (4,)), ("none", (5,)), ("silu", (6,)),
           ("norm256", (7,)), ("silu", (8,)), ("sigmoid", (9,)), ("sigmoid", (10,)), ("sigmoid", (11,)))
    U, C_GATE, D_Q, D_K, D_V, D_GATE, X_Q, X_GATE, MERGE = range(9)
    gain_rows = {D_Q: jnp.tile(diff_qn_g, d // DIFF_HEAD_DIM) * (DIFF_HEAD_DIM ** -0.5 * LOG2_E),
                 D_K: jnp.tile(diff_kn_g, d // DIFF_HEAD_DIM),
                 X_Q: jnp.tile(x_qn_g, X_HEADS) * (x_head_dim ** -0.5)}
    gains = jnp.stack([gain_rows.get(b, ones) for b in range(len(ops))]).reshape(len(ops), 1, d)
    acts = _norm_proj(x2d, norm_g, w_in.astype(BF16), ops, gains, g64, g256, tm=256)

    mem_ops = (("norm256", (0,)), ("none", (1,)))
    mem_gains = jnp.stack([jnp.tile(x_kn_g, X_HEADS), ones]).reshape(2, 1, d)
    mem_kv = _norm_proj(mem.reshape(batch * mem_len, d), mem_norm_g, w_mem_kv.astype(BF16), mem_ops, mem_gains,
                        g64, g256, tm=256)

    conv_act = _conv_branch(acts, U, C_GATE, conv_dw, conv_dw_b, conv_ln_g, conv_ln_b, batch, seq, ts=256)
    lam_vecs = jnp.stack([lambda_q1, lambda_k1, lambda_q2, lambda_k2])
    diff_act = _diff_attention(acts, (D_Q, D_K, D_V, D_GATE), attn_tables, lam_vecs, diff_subln_g,
                               batch, seq, heads, tq=256, lambda_init=lambda_init)
    x_act = _cross_attention(acts, X_Q, X_GATE, mem_kv, batch, seq, mem_len, X_HEADS, tq=512)

    out = _merge_out(x2d, conv_act, diff_act, x_act, acts, MERGE,
                     w_conv_proj.astype(BF16), w_diff_proj.astype(BF16), w_x_proj.astype(BF16),
                     w_out.astype(BF16), tm=512)
    return out.reshape(batch, seq, d)


def kernel(x, mem, norm_g, mem_norm_g, w_in, conv_dw, conv_dw_b, conv_ln_g, conv_ln_b, w_conv_proj, diff_qn_g, diff_kn_g, lambda_q1, lambda_k1, lambda_q2, lambda_k2, diff_subln_g, w_diff_proj, w_mem_kv, x_qn_g, x_kn_g, w_x_proj, w_out):
    params = (norm_g, mem_norm_g, w_in, conv_dw, conv_dw_b, conv_ln_g, conv_ln_b, w_conv_proj, diff_qn_g,
              diff_kn_g, lambda_q1, lambda_k1, lambda_q2, lambda_k2, diff_subln_g, w_diff_proj, w_mem_kv,
              x_qn_g, x_kn_g, w_x_proj, w_out)
    for l in range(norm_g.shape[0]):
        x = _layer(x, mem, l, *(p[l] for p in params))
    return x
```

```python
import functools
import math

import jax
import jax.numpy as jnp
from jax import lax
from jax.experimental import pallas as pl
from jax.experimental.pallas import tpu as pltpu

CONV_K = 31
DIFF_HEAD_DIM = 64
DIFF_V_DIM = 2 * DIFF_HEAD_DIM
X_HEADS = 4
N_BRANCH = 3
RMS_EPS = 1e-6
LN_EPS = 1e-5
MASK_VALUE = -1e30
LOG2_E = math.log2(math.e)

V7X_LANES = 128
V7X_SUBLANES = 8
V7X_MXU_DIM = 256
V7X_VMEM_LIMIT_BYTES = 56 * 1024 * 1024

BF16 = jnp.bfloat16
F32 = jnp.float32


def _cparams(sem):
    return pltpu.CompilerParams(dimension_semantics=sem, vmem_limit_bytes=V7X_VMEM_LIMIT_BYTES)


def _sigmoid(x):
    return 1.0 / (1.0 + jnp.exp(-x))


def _silu(x):
    return x * _sigmoid(x)


def _nt_dot(a, b):
    return lax.dot_general(a, b, (((1,), (1,)), ((), ())), preferred_element_type=F32)


PROJ_TN = 1024


def _group_rms(x, gmat, group, gain):
    x2 = (x * x).astype(BF16)
    n = x.shape[1]
    ss = jnp.concatenate([jnp.dot(x2[:, c:c + V7X_MXU_DIM], gmat, preferred_element_type=F32)
                          for c in range(0, n, V7X_MXU_DIM)], axis=1)
    return x * lax.rsqrt(ss * (1.0 / group) + RMS_EPS) * gain


def _norm_proj_kernel(x_ref, g_ref, w_ref, gain_ref, g64_ref, g256_ref, o_ref, *, ops):
    x = x_ref[...]
    h = (x * lax.rsqrt(jnp.mean(x * x, axis=-1, keepdims=True) + RMS_EPS) * g_ref[...]).astype(BF16)

    def proj(blk):
        return jnp.dot(h, w_ref[:, blk * PROJ_TN:(blk + 1) * PROJ_TN], preferred_element_type=F32)

    for ob, (kind, blks) in enumerate(ops):
        if kind == "glu":
            y = proj(blks[0]) * _sigmoid(proj(blks[1]))
        elif kind == "silu":
            y = _silu(proj(blks[0]))
        elif kind == "sigmoid":
            y = _sigmoid(proj(blks[0]))
        elif kind == "norm64":
            y = _group_rms(proj(blks[0]), g64_ref[...], DIFF_HEAD_DIM, gain_ref[ob])
        elif kind == "norm256":
            y = _group_rms(proj(blks[0]), g256_ref[...], V7X_MXU_DIM, gain_ref[ob])
        else:
            assert kind == "none", kind
            y = proj(blks[0])
        o_ref[:, ob * PROJ_TN:(ob + 1) * PROJ_TN] = y.astype(o_ref.dtype)


def _norm_proj(x2d, g, w, ops, gains, g64, g256, tm):
    n, d = x2d.shape
    nout = len(ops) * PROJ_TN

    def resident(a):
        return pl.BlockSpec(a.shape, lambda i: (0,) * a.ndim, pipeline_mode=pl.Buffered(1))

    g2 = g.reshape(1, d)
    return pl.pallas_call(
        functools.partial(_norm_proj_kernel, ops=ops),
        grid=(n // tm,),
        in_specs=[pl.BlockSpec((tm, d), lambda i: (i, 0)),
                  resident(g2), resident(w), resident(gains), resident(g64), resident(g256)],
        out_specs=pl.BlockSpec((tm, nout), lambda i: (i, 0)),
        out_shape=jax.ShapeDtypeStruct((n, nout), BF16),
        compiler_params=_cparams(("parallel",)),
        name="norm_proj",
    )(x2d, g2, w, gains, g64, g256)


def _group_ones(group):
    r = jnp.arange(V7X_MXU_DIM) // group
    return (r[:, None] == r[None, :]).astype(BF16)


CONV_HALO = 32
CONV_ROWS = 32
CONV_COLS = 512


def _conv_kernel(u_ref, gate_ref, dw_ref, dwb_ref, lng_ref, lnb_ref, o_ref, win_ref, y_ref, *, ts):
    j = pl.program_id(1)
    t0 = pl.multiple_of(j * ts, ts)
    d = u_ref.shape[1]

    @pl.when(j == 0)
    def _():
        win_ref[0, 0:CONV_HALO, :] = jnp.zeros((CONV_HALO, d), F32)

    @pl.when(j > 0)
    def _():
        win_ref[0, 0:CONV_HALO, :] = u_ref[pl.ds(t0 - CONV_HALO, CONV_HALO), :].astype(F32)

    win_ref[0, CONV_HALO:CONV_HALO + ts, :] = u_ref[pl.ds(t0, ts), :].astype(F32)

    first = CONV_HALO - (CONV_K - 1)
    span = CONV_HALO + ts - V7X_SUBLANES
    for c in range(d // V7X_LANES):
        cs = slice(c * V7X_LANES, (c + 1) * V7X_LANES)
        x = win_ref[0, :, cs]
        for s in range(1, V7X_SUBLANES):
            x = pltpu.roll(x, x.shape[0] - 1, 0)
            win_ref[s, 0:span, cs] = x[0:span]

    for r0 in range(0, ts, CONV_ROWS):
        for c in range(d // CONV_COLS):
            cs = slice(c * CONV_COLS, (c + 1) * CONV_COLS)
            acc = jnp.zeros((CONV_ROWS, CONV_COLS), F32)
            for s in range(V7X_SUBLANES):
                taps = [k for k in range(CONV_K) if (first + k) % V7X_SUBLANES == s]
                lo = r0 + first + taps[0] - s
                big = win_ref[s, lo:lo + (taps[-1] - taps[0]) + CONV_ROWS, cs]
                for k in taps:
                    w = jnp.concatenate([dw_ref[k, :, cs]] * (CONV_ROWS // V7X_SUBLANES), axis=0)
                    acc = acc + big[k - taps[0]:k - taps[0] + CONV_ROWS] * w
            y_ref[r0:r0 + CONV_ROWS, cs] = acc + dwb_ref[:, cs]

    y = y_ref[...]
    mu = jnp.mean(y, axis=-1, keepdims=True)
    yc = y - mu
    yn = yc * lax.rsqrt(jnp.mean(yc * yc, axis=-1, keepdims=True) + LN_EPS)
    yn = yn * lng_ref[...] + lnb_ref[...]
    o_ref[...] = (_silu(yn) * gate_ref[...].astype(F32)).astype(o_ref.dtype)


def _conv_branch(acts, u_block, gate_block, dw, dwb, lng, lnb, batch, seq, ts):
    n, d = acts.shape[0], dw.shape[1]
    nt = seq // ts
    return pl.pallas_call(
        functools.partial(_conv_kernel, ts=ts),
        grid=(batch, nt),
        in_specs=[pl.BlockSpec((seq, d), lambda b, j: (b, u_block)),
                  pl.BlockSpec((ts, d), lambda b, j: (b * nt + j, gate_block)),
                  pl.BlockSpec((CONV_K, V7X_SUBLANES, d), lambda b, j: (0, 0, 0)),
                  pl.BlockSpec((1, d), lambda b, j: (0, 0)),
                  pl.BlockSpec((1, d), lambda b, j: (0, 0)),
                  pl.BlockSpec((1, d), lambda b, j: (0, 0))],
        out_specs=pl.BlockSpec((ts, d), lambda b, j: (b * nt + j, 0)),
        out_shape=jax.ShapeDtypeStruct((n, d), BF16),
        scratch_shapes=[pltpu.VMEM((V7X_SUBLANES, CONV_HALO + ts, d), F32), pltpu.VMEM((ts, d), F32)],
        compiler_params=_cparams(("parallel", "arbitrary")),
        name="conv_branch",
    )(acts, acts, jnp.broadcast_to(dw[:, None, :], (CONV_K, V7X_SUBLANES, d)),
      dwb.reshape(1, d), lng.reshape(1, d), lnb.reshape(1, d))


EXTRA_ROWS = 16
PIECES = 3
POS_RADIX = 128
BF16_NORM_MARGIN = 1.02
MAX_FAST_BOUND = 40.0


def _pieces(v):
    out = []
    for _ in range(PIECES):
        piece = v.astype(BF16).astype(F32)
        out.append(piece)
        v = v - piece
    return out


def _attn_tables(heads, seq, qn_g, kn_g):
    d = DIFF_HEAD_DIM
    slopes = _pieces(2.0 ** (-8.0 * jnp.arange(1, heads + 1, dtype=F32) / heads) * LOG2_E)
    bound = (d ** 0.5 * LOG2_E * BF16_NORM_MARGIN) * jnp.max(jnp.abs(qn_g)) * jnp.max(jnp.abs(kn_g))
    fast = bound < MAX_FAST_BOUND
    neg_b = _pieces(jnp.broadcast_to(jnp.where(fast, -bound, 0.0), (heads,)))
    pos = jnp.arange(seq)
    hi = jnp.broadcast_to((pos // POS_RADIX).astype(F32), (heads, seq))
    lo = jnp.broadcast_to((pos % POS_RADIX).astype(F32), (heads, seq))
    const = lambda v: jnp.broadcast_to(v[:, None], (heads, seq))
    zero = jnp.zeros((heads, seq), F32)
    q_rows = ([const(POS_RADIX * p) for p in slopes] + [const(p) for p in slopes] + [hi] * PIECES + [lo] * PIECES
              + [const(p) for p in neg_b] + [zero])
    k_rows = ([hi] * PIECES + [lo] * PIECES + [const(-POS_RADIX * p) for p in slopes] + [const(-p) for p in slopes]
              + [zero + 1.0] * PIECES + [zero])
    q_extra = jnp.stack(q_rows, axis=1).astype(BF16)
    k_extra = jnp.stack(k_rows, axis=2).astype(BF16)
    lanes = lambda before: ((0, 0), (0, 0), (before, DIFF_V_DIM - EXTRA_ROWS - before))
    k_tab = jnp.stack([jnp.pad(k_extra, lanes(d)), jnp.pad(k_extra, lanes(0))], axis=1)
    return q_extra, k_tab, fast.astype(jnp.int32).reshape(1)


def _diff_attn_kernel(fast_ref, qx_ref, kx_ref, q_ref, k_ref, v_ref, gate_ref, lam_ref, subg_ref, o_ref,
                      km_ref, vt_ref, qt_ref, mask_ref, *, tq, lambda_init):
    seq = q_ref.shape[0]
    nq = seq // tq
    d = DIFF_HEAD_DIM
    first_half = lax.broadcasted_iota(jnp.int32, (tq, DIFF_V_DIM), 1) < d

    for c in range(nq):
        rows = slice(c * tq, (c + 1) * tq)
        k = k_ref[rows, :]
        km_ref[0, rows, :] = jnp.where(first_half, k, kx_ref[0, rows, :])
        km_ref[1, rows, :] = jnp.where(first_half, kx_ref[1, rows, :], k)
        vt_ref[:, rows] = v_ref[rows, :].astype(F32).T.astype(vt_ref.dtype)

    pad = jnp.zeros((DIFF_V_DIM - d - EXTRA_ROWS, tq), F32)
    for i in range(nq):
        cols = slice(i * tq, (i + 1) * tq)
        extra = qx_ref[:, cols].astype(F32)
        qt = q_ref[cols, :].astype(F32).T
        qt_ref[i, 0] = jnp.concatenate([qt[:d], extra, pad], axis=0).astype(qt_ref.dtype)
        qt_ref[i, 1] = jnp.concatenate([extra, pad, qt[d:]], axis=0).astype(qt_ref.dtype)

    kk = lax.broadcasted_iota(jnp.int32, (tq, tq), 0)
    qq = lax.broadcasted_iota(jnp.int32, (tq, tq), 1)
    mask_ref[...] = jnp.where(kk <= qq, 0.0, MASK_VALUE)

    lam_v = lam_ref[...]
    lam = (jnp.exp(jnp.sum(lam_v[0:1] * lam_v[1:2], axis=-1, keepdims=True))
           - jnp.exp(jnp.sum(lam_v[2:3] * lam_v[3:4], axis=-1, keepdims=True)) + lambda_init)

    def scores(i):
        keys = (i + 1) * tq
        return [jnp.dot(km_ref[mp, :keys, :], qt_ref[i, mp], preferred_element_type=F32) for mp in range(2)]

    def attend(bounded):
        x_next = scores(0)
        for i in range(nq):
            x_cur = x_next
            if i + 1 < nq:
                x_next = scores(i + 1)
            keys = (i + 1) * tq
            vt = vt_ref[:, :keys]
            heads_out = []
            for x in x_cur:
                x = (jnp.concatenate([x[:keys - tq], x[keys - tq:] + mask_ref[...]], axis=0) if i
                     else x + mask_ref[...])
                p = jnp.exp2(x) if bounded else jnp.exp2(x - jnp.max(x, axis=0, keepdims=True))
                pv = jnp.dot(vt, p.astype(vt.dtype), preferred_element_type=F32)
                heads_out.append(pv / jnp.sum(p, axis=0, keepdims=True))
            rows = slice(i * tq, (i + 1) * tq)
            o = heads_out[0] - lam * heads_out[1]
            o = o * lax.rsqrt(jnp.mean(o * o, axis=0, keepdims=True) + RMS_EPS)
            o = o.T * (subg_ref[...] * (1.0 - lambda_init))
            o_ref[rows, :] = (o * gate_ref[rows, :].astype(F32)).astype(o_ref.dtype)

    fast = fast_ref[0] == 1
    pl.when(fast)(functools.partial(attend, True))
    pl.when(jnp.logical_not(fast))(functools.partial(attend, False))


def _diff_attention(acts, blocks, tables, lam_vecs, subg, batch, seq, heads, tq, lambda_init):
    n = acts.shape[0]
    nq = seq // tq
    q_extra, k_tab, fast = tables
    qb, kb, vb, gb = (b * heads for b in blocks)
    kernel = functools.partial(_diff_attn_kernel, tq=tq, lambda_init=lambda_init)
    head_spec = lambda first: pl.BlockSpec((seq, DIFF_V_DIM), lambda b, h: (b, first + h))
    return pl.pallas_call(
        kernel,
        grid=(batch, heads),
        in_specs=[pl.BlockSpec(memory_space=pltpu.SMEM),
                  pl.BlockSpec((None, EXTRA_ROWS, seq), lambda b, h: (h, 0, 0)),
                  pl.BlockSpec((None, 2, seq, DIFF_V_DIM), lambda b, h: (h, 0, 0, 0)),
                  head_spec(qb), head_spec(kb), head_spec(vb), head_spec(gb),
                  pl.BlockSpec((4, DIFF_HEAD_DIM), lambda b, h: (0, 0)),
                  pl.BlockSpec((1, DIFF_V_DIM), lambda b, h: (0, 0))],
        out_specs=head_spec(0),
        out_shape=jax.ShapeDtypeStruct((n, heads * DIFF_V_DIM), BF16),
        scratch_shapes=[pltpu.VMEM((2, seq, DIFF_V_DIM), BF16),
                        pltpu.VMEM((DIFF_V_DIM, seq), BF16),
                        pltpu.VMEM((nq, 2, DIFF_V_DIM, tq), BF16),
                        pltpu.VMEM((tq, tq), F32)],
        compiler_params=_cparams(("parallel", "parallel")),
        name="diff_attention",
    )(fast, q_extra, k_tab, acts, acts, acts, acts, lam_vecs, subg.reshape(1, DIFF_V_DIM))


def _xattn_kernel(q_ref, k_ref, v_ref, gate_ref, o_ref, *, heads):
    hd = q_ref.shape[1] // heads
    for h in range(heads):
        sl = slice(h * hd, (h + 1) * hd)
        s = _nt_dot(q_ref[:, sl], k_ref[:, sl])
        m = jnp.max(s, axis=-1, keepdims=True)
        p = jnp.exp(s - m)
        l = jnp.sum(p, axis=-1, keepdims=True)
        o = jnp.dot(p.astype(BF16), v_ref[:, sl], preferred_element_type=F32) / l
        o_ref[:, sl] = (o * gate_ref[:, sl].astype(F32)).astype(o_ref.dtype)


def _cross_attention(acts, q_block, gate_block, mem_kv, batch, seq, mem_len, heads, tq):
    n = acts.shape[0]
    d = mem_kv.shape[1] // 2
    nq = seq // tq
    return pl.pallas_call(
        functools.partial(_xattn_kernel, heads=heads),
        grid=(batch, nq),
        in_specs=[pl.BlockSpec((tq, d), lambda b, i: (b * nq + i, q_block)),
                  pl.BlockSpec((mem_len, d), lambda b, i: (b, 0)),
                  pl.BlockSpec((mem_len, d), lambda b, i: (b, 1)),
                  pl.BlockSpec((tq, d), lambda b, i: (b * nq + i, gate_block))],
        out_specs=pl.BlockSpec((tq, d), lambda b, i: (b * nq + i, 0)),
        out_shape=jax.ShapeDtypeStruct((n, d), BF16),
        compiler_params=_cparams(("parallel", "arbitrary")),
        name="cross_attention",
    )(acts, mem_kv, mem_kv, acts)


def _merge_kernel(x_ref, ca_ref, da_ref, xa_ref, g0_ref, g1_ref, g2_ref, wc_ref, wd_ref, wx_ref, wo_ref, o_ref,
                  wb_ref):
    @pl.when(pl.program_id(0) == 0)
    def _():
        for c, w_ref in enumerate((wc_ref, wd_ref, wx_ref, wo_ref)):
            wb_ref[c] = w_ref[...].astype(wb_ref.dtype)

    y = g0_ref[...].astype(F32) * jnp.dot(ca_ref[...], wb_ref[0], preferred_element_type=F32)
    y = y + g1_ref[...].astype(F32) * jnp.dot(da_ref[...], wb_ref[1], preferred_element_type=F32)
    y = y + g2_ref[...].astype(F32) * jnp.dot(xa_ref[...], wb_ref[2], preferred_element_type=F32)
    o_ref[...] = x_ref[...] + jnp.dot(y.astype(BF16), wb_ref[3], preferred_element_type=F32)


def _merge_out(x2d, ca, da, xa, acts, gate_block0, wc, wd, wx, wo, tm):
    n, d = x2d.shape
    row = lambda i: (i, 0)
    fixed = lambda i: (0, 0)
    act_spec = pl.BlockSpec((tm, d), row)
    w_spec = pl.BlockSpec((d, d), fixed, pipeline_mode=pl.Buffered(1))
    gate_specs = [pl.BlockSpec((tm, d), functools.partial(lambda i, c: (i, c), c=gate_block0 + c))
                  for c in range(N_BRANCH)]
    return pl.pallas_call(
        _merge_kernel,
        grid=(n // tm,),
        in_specs=[pl.BlockSpec((tm, d), row), act_spec, act_spec, act_spec, *gate_specs,
                  w_spec, w_spec, w_spec, w_spec],
        out_specs=pl.BlockSpec((tm, d), row),
        out_shape=jax.ShapeDtypeStruct((n, d), x2d.dtype),
        scratch_shapes=[pltpu.VMEM((4, d, d), BF16)],
        compiler_params=_cparams(("arbitrary",)),
        name="merge_out",
    )(x2d, ca, da, xa, acts, acts, acts, wc, wd, wx, wo)


def _layer(x, mem, l, norm_g, mem_norm_g, w_in, conv_dw, conv_dw_b, conv_ln_g, conv_ln_b, w_conv_proj,
           diff_qn_g, diff_kn_g, lambda_q1, lambda_k1, lambda_q2, lambda_k2, diff_subln_g, w_diff_proj,
           w_mem_kv, x_qn_g, x_kn_g, w_x_proj, w_out):
    batch, seq, d = x.shape
    mem_len = mem.shape[1]
    heads = d // DIFF_V_DIM
    x_head_dim = d // X_HEADS
    assert x_head_dim == V7X_MXU_DIM and d % V7X_MXU_DIM == 0
    n = batch * seq
    x2d = x.reshape(n, d)
    lambda_init = 0.8 - 0.6 * math.exp(-0.3 * l)
    attn_tables = _attn_tables(heads, seq, diff_qn_g, diff_kn_g)

    assert d == PROJ_TN
    g64 = _group_ones(DIFF_HEAD_DIM)
    g256 = _group_ones(V7X_MXU_DIM)
    ones = jnp.ones((d,), F32)

    ops = (("glu", (0, 1)), ("silu", (2,)), ("norm64", (3,)), ("norm64", (4,)), ("none", (5,)), ("silu", (6,)),
           ("norm256", (7,)), ("silu", (8,)), ("sigmoid", (9,)), ("sigmoid", (10,)), ("sigmoid", (11,)))
    U, C_GATE, D_Q, D_K, D_V, D_GATE, X_Q, X_GATE, MERGE = range(9)
    gain_rows = {D_Q: jnp.tile(diff_qn_g, d // DIFF_HEAD_DIM) * (DIFF_HEAD_DIM ** -0.5 * LOG2_E),
                 D_K: jnp.tile(diff_kn_g, d // DIFF_HEAD_DIM),
                 X_Q: jnp.tile(x_qn_g, X_HEADS) * (x_head_dim ** -0.5)}
    gains = jnp.stack([gain_rows.get(b, ones) for b in range(len(ops))]).reshape(len(ops), 1, d)
    acts = _norm_proj(x2d, norm_g, w_in.astype(BF16), ops, gains, g64, g256, tm=256)

    mem_ops = (("norm256", (0,)), ("none", (1,)))
    mem_gains = jnp.stack([jnp.tile(x_kn_g, X_HEADS), ones]).reshape(2, 1, d)
    mem_kv = _norm_proj(mem.reshape(batch * mem_len, d), mem_norm_g, w_mem_kv.astype(BF16), mem_ops, mem_gains,
                        g64, g256, tm=256)

    conv_act = _conv_branch(acts, U, C_GATE, conv_dw, conv_dw_b, conv_ln_g, conv_ln_b, batch, seq, ts=256)
    lam_vecs = jnp.stack([lambda_q1, lambda_k1, lambda_q2, lambda_k2])
    diff_act = _diff_attention(acts, (D_Q, D_K, D_V, D_GATE), attn_tables, lam_vecs, diff_subln_g,
                               batch, seq, heads, tq=256, lambda_init=lambda_init)
    x_act = _cross_attention(acts, X_Q, X_GATE, mem_kv, batch, seq, mem_len, X_HEADS, tq=512)

    out = _merge_out(x2d, conv_act, diff_act, x_act, acts, MERGE, w_conv_proj, w_diff_proj, w_x_proj, w_out, tm=512)
    return out.reshape(batch, seq, d)


def kernel(x, mem, norm_g, mem_norm_g, w_in, conv_dw, conv_dw_b, conv_ln_g, conv_ln_b, w_conv_proj, diff_qn_g, diff_kn_g, lambda_q1, lambda_k1, lambda_q2, lambda_k2, diff_subln_g, w_diff_proj, w_mem_kv, x_qn_g, x_kn_g, w_x_proj, w_out):
    params = (norm_g, mem_norm_g, w_in, conv_dw, conv_dw_b, conv_ln_g, conv_ln_b, w_conv_proj, diff_qn_g,
              diff_kn_g, lambda_q1, lambda_k1, lambda_q2, lambda_k2, diff_subln_g, w_diff_proj, w_mem_kv,
              x_qn_g, x_kn_g, w_x_proj, w_out)
    for l in range(norm_g.shape[0]):
        x = _layer(x, mem, l, *(p[l] for p in params))
    return x
```

```python
import functools
import math

import jax
import jax.numpy as jnp
from jax import lax
import numpy as np
from jax.experimental import pallas as pl
from jax.experimental.pallas import tpu as pltpu

CONV_K = 31
DIFF_HEAD_DIM = 64
DIFF_V_DIM = 2 * DIFF_HEAD_DIM
X_HEADS = 4
N_BRANCH = 3
RMS_EPS = 1e-6
LN_EPS = 1e-5
MASK_VALUE = -1e30
LOG2_E = math.log2(math.e)

V7X_LANES = 128
V7X_SUBLANES = 8
V7X_MXU_DIM = 256
V7X_VMEM_LIMIT_BYTES = 56 * 1024 * 1024

BF16 = jnp.bfloat16
F32 = jnp.float32


def _cparams(sem):
    return pltpu.CompilerParams(dimension_semantics=sem, vmem_limit_bytes=V7X_VMEM_LIMIT_BYTES)


def _sigmoid(x):
    return 1.0 / (1.0 + jnp.exp(-x))


def _silu(x):
    return x * _sigmoid(x)


def _nt_dot(a, b):
    return lax.dot_general(a, b, (((1,), (1,)), ((), ())), preferred_element_type=F32)


PROJ_TN = 1024


def _group_rms(x, gmat, group, gain):
    x2 = (x * x).astype(BF16)
    n = x.shape[1]
    ss = jnp.concatenate([jnp.dot(x2[:, c:c + V7X_MXU_DIM], gmat, preferred_element_type=F32)
                          for c in range(0, n, V7X_MXU_DIM)], axis=1)
    return x * lax.rsqrt(ss * (1.0 / group) + RMS_EPS) * gain


def _norm_proj_kernel(x_ref, g_ref, w_ref, gain_ref, g64_ref, g256_ref, o_ref, *wb, ops):
    if wb:
        w_f32, (w_ref,) = w_ref, wb

        @pl.when(pl.program_id(0) == 0)
        def _():
            w_ref[...] = w_f32[...].astype(w_ref.dtype)

    x = x_ref[...]
    h = (x * lax.rsqrt(jnp.mean(x * x, axis=-1, keepdims=True) + RMS_EPS) * g_ref[...]).astype(BF16)

    def proj(blk):
        return jnp.dot(h, w_ref[:, blk * PROJ_TN:(blk + 1) * PROJ_TN], preferred_element_type=F32)

    for ob, (kind, blks) in enumerate(ops):
        if kind == "glu":
            y = proj(blks[0]) * _sigmoid(proj(blks[1]))
        elif kind == "silu":
            y = _silu(proj(blks[0]))
        elif kind == "sigmoid":
            y = _sigmoid(proj(blks[0]))
        elif kind == "norm64":
            y = _group_rms(proj(blks[0]), g64_ref[...], DIFF_HEAD_DIM, gain_ref[ob])
        elif kind == "norm256":
            y = _group_rms(proj(blks[0]), g256_ref[...], V7X_MXU_DIM, gain_ref[ob])
        else:
            assert kind == "none", kind
            y = proj(blks[0])
        o_ref[:, ob * PROJ_TN:(ob + 1) * PROJ_TN] = y.astype(o_ref.dtype)


def _norm_proj(x2d, g, w, ops, gains, g64, g256, tm):
    n, d = x2d.shape
    nout = len(ops) * PROJ_TN

    def resident(a):
        return pl.BlockSpec(a.shape, lambda i: (0,) * a.ndim, pipeline_mode=pl.Buffered(1))

    g2 = g.reshape(1, d)
    cast_in_kernel = w.dtype != BF16
    return pl.pallas_call(
        functools.partial(_norm_proj_kernel, ops=ops),
        grid=(n // tm,),
        in_specs=[pl.BlockSpec((tm, d), lambda i: (i, 0)),
                  resident(g2), resident(w), resident(gains), resident(g64), resident(g256)],
        out_specs=pl.BlockSpec((tm, nout), lambda i: (i, 0)),
        out_shape=jax.ShapeDtypeStruct((n, nout), BF16),
        scratch_shapes=[pltpu.VMEM(w.shape, BF16)] if cast_in_kernel else [],
        compiler_params=_cparams(("arbitrary" if cast_in_kernel else "parallel",)),
        name="norm_proj",
    )(x2d, g2, w, gains, g64, g256)


def _group_ones(group):
    r = jnp.arange(V7X_MXU_DIM) // group
    return (r[:, None] == r[None, :]).astype(BF16)


CONV_HALO = 32
CONV_ROWS = 32
CONV_COLS = 512


def _conv_kernel(u_ref, gate_ref, dw_ref, dwb_ref, lng_ref, lnb_ref, o_ref, win_ref, y_ref, *, ts):
    j = pl.program_id(1)
    t0 = pl.multiple_of(j * ts, ts)
    d = u_ref.shape[1]

    @pl.when(j == 0)
    def _():
        win_ref[0, 0:CONV_HALO, :] = jnp.zeros((CONV_HALO, d), F32)

    @pl.when(j > 0)
    def _():
        win_ref[0, 0:CONV_HALO, :] = u_ref[pl.ds(t0 - CONV_HALO, CONV_HALO), :].astype(F32)

    win_ref[0, CONV_HALO:CONV_HALO + ts, :] = u_ref[pl.ds(t0, ts), :].astype(F32)

    first = CONV_HALO - (CONV_K - 1)
    span = CONV_HALO + ts - V7X_SUBLANES
    for c in range(d // V7X_LANES):
        cs = slice(c * V7X_LANES, (c + 1) * V7X_LANES)
        x = win_ref[0, :, cs]
        for s in range(1, V7X_SUBLANES):
            x = pltpu.roll(x, x.shape[0] - 1, 0)
            win_ref[s, 0:span, cs] = x[0:span]

    for r0 in range(0, ts, CONV_ROWS):
        for c in range(d // CONV_COLS):
            cs = slice(c * CONV_COLS, (c + 1) * CONV_COLS)
            acc = jnp.zeros((CONV_ROWS, CONV_COLS), F32)
            for s in range(V7X_SUBLANES):
                taps = [k for k in range(CONV_K) if (first + k) % V7X_SUBLANES == s]
                lo = r0 + first + taps[0] - s
                big = win_ref[s, lo:lo + (taps[-1] - taps[0]) + CONV_ROWS, cs]
                for k in taps:
                    w = jnp.concatenate([dw_ref[k, :, cs]] * (CONV_ROWS // V7X_SUBLANES), axis=0)
                    acc = acc + big[k - taps[0]:k - taps[0] + CONV_ROWS] * w
            y_ref[r0:r0 + CONV_ROWS, cs] = acc + dwb_ref[:, cs]

    y = y_ref[...]
    mu = jnp.mean(y, axis=-1, keepdims=True)
    yc = y - mu
    yn = yc * lax.rsqrt(jnp.mean(yc * yc, axis=-1, keepdims=True) + LN_EPS)
    yn = yn * lng_ref[...] + lnb_ref[...]
    o_ref[...] = (_silu(yn) * gate_ref[...].astype(F32)).astype(o_ref.dtype)


def _conv_branch(acts, u_block, gate_block, dw, dwb, lng, lnb, batch, seq, ts):
    n, d = acts.shape[0], dw.shape[1]
    nt = seq // ts
    return pl.pallas_call(
        functools.partial(_conv_kernel, ts=ts),
        grid=(batch, nt),
        in_specs=[pl.BlockSpec((seq, d), lambda b, j: (b, u_block)),
                  pl.BlockSpec((ts, d), lambda b, j: (b * nt + j, gate_block)),
                  pl.BlockSpec((CONV_K, V7X_SUBLANES, d), lambda b, j: (0, 0, 0)),
                  pl.BlockSpec((1, d), lambda b, j: (0, 0)),
                  pl.BlockSpec((1, d), lambda b, j: (0, 0)),
                  pl.BlockSpec((1, d), lambda b, j: (0, 0))],
        out_specs=pl.BlockSpec((ts, d), lambda b, j: (b * nt + j, 0)),
        out_shape=jax.ShapeDtypeStruct((n, d), BF16),
        scratch_shapes=[pltpu.VMEM((V7X_SUBLANES, CONV_HALO + ts, d), F32), pltpu.VMEM((ts, d), F32)],
        compiler_params=_cparams(("parallel", "arbitrary")),
        name="conv_branch",
    )(acts, acts, jnp.broadcast_to(dw[:, None, :], (CONV_K, V7X_SUBLANES, d)),
      dwb.reshape(1, d), lng.reshape(1, d), lnb.reshape(1, d))


EXTRA_ROWS = 16
PIECES = 3
POS_RADIX = 128
BF16_NORM_MARGIN = 1.02
MAX_FAST_BOUND = 40.0


def _pieces(v):
    out = []
    for _ in range(PIECES):
        piece = v.astype(BF16).astype(F32)
        out.append(piece)
        v = v - piece
    return out


def _position_tables(heads, seq):
    d = DIFF_HEAD_DIM
    rest = (2.0 ** (-8.0 * np.arange(1, heads + 1, dtype=np.float32) / heads) * np.float32(LOG2_E)).astype(np.float32)
    slopes = []
    for _ in range(PIECES):
        piece = rest.astype(BF16).astype(np.float32)
        slopes.append(piece)
        rest = rest - piece
    pos = np.arange(seq)
    hi = np.broadcast_to((pos // POS_RADIX).astype(np.float32), (heads, seq))
    lo = np.broadcast_to((pos % POS_RADIX).astype(np.float32), (heads, seq))
    const = lambda v: np.broadcast_to(v[:, None], (heads, seq))
    zero = np.zeros((heads, seq), np.float32)
    q_rows = ([const(POS_RADIX * p) for p in slopes] + [const(p) for p in slopes] + [hi] * PIECES + [lo] * PIECES
              + [zero] * (PIECES + 1))
    k_rows = ([hi] * PIECES + [lo] * PIECES + [const(-POS_RADIX * p) for p in slopes] + [const(-p) for p in slopes]
              + [zero + 1.0] * PIECES + [zero])
    q_extra = np.stack(q_rows, axis=1)
    k_extra = np.stack(k_rows, axis=2)
    k_tab = np.zeros((heads, 2, seq, DIFF_V_DIM), np.float32)
    k_tab[:, 0, :, d:d + EXTRA_ROWS] = k_extra
    k_tab[:, 1, :, :EXTRA_ROWS] = k_extra
    return jnp.asarray(q_extra.astype(BF16)), jnp.asarray(k_tab.astype(BF16))


def _score_bound(qn_g, kn_g):
    bound = (DIFF_HEAD_DIM ** 0.5 * LOG2_E * BF16_NORM_MARGIN) * jnp.max(jnp.abs(qn_g)) * jnp.max(jnp.abs(kn_g))
    fast = bound < MAX_FAST_BOUND
    neg_b = _pieces(jnp.where(fast, -bound, 0.0))
    rows = [jnp.zeros((), F32)] * (4 * PIECES) + neg_b + [jnp.zeros((), F32)]
    tile = jnp.broadcast_to(jnp.stack(rows)[:, None], (EXTRA_ROWS, V7X_LANES))
    return tile, fast.astype(jnp.int32).reshape(1)


HEADS_PER_STEP = 2


def _diff_attn_kernel(fast_ref, bound_ref, qx_ref, kx_ref, q_ref, k_ref, v_ref, gate_ref, lam_ref, subg_ref, o_ref,
                      km_ref, vt_ref, qt_ref, mask_ref, *, tq, lambda_init):
    seq = q_ref.shape[0]
    nq = seq // tq
    d = DIFF_HEAD_DIM
    first_half = lax.broadcasted_iota(jnp.int32, (tq, DIFF_V_DIM), 1) < d
    pad = jnp.zeros((DIFF_V_DIM - d - EXTRA_ROWS, tq), F32)
    bound_rows = jnp.concatenate([bound_ref[...]] * (tq // V7X_LANES), axis=1)

    for hh in range(HEADS_PER_STEP):
        hl = slice(hh * DIFF_V_DIM, (hh + 1) * DIFF_V_DIM)
        for c in range(nq):
            rows = slice(c * tq, (c + 1) * tq)
            k = k_ref[rows, hl]
            km_ref[hh, 0, rows, :] = jnp.where(first_half, k, kx_ref[hh, 0, rows, :])
            km_ref[hh, 1, rows, :] = jnp.where(first_half, kx_ref[hh, 1, rows, :], k)
            vt_ref[hh, :, rows] = v_ref[rows, hl].astype(F32).T.astype(vt_ref.dtype)
        for i in range(nq):
            cols = slice(i * tq, (i + 1) * tq)
            extra = qx_ref[hh, :, cols].astype(F32) + bound_rows
            qt = q_ref[cols, hl].astype(F32).T
            qt_ref[hh, i, 0] = jnp.concatenate([qt[:d], extra, pad], axis=0).astype(qt_ref.dtype)
            qt_ref[hh, i, 1] = jnp.concatenate([extra, pad, qt[d:]], axis=0).astype(qt_ref.dtype)

    kk = lax.broadcasted_iota(jnp.int32, (tq, tq), 0)
    qq = lax.broadcasted_iota(jnp.int32, (tq, tq), 1)
    mask_ref[...] = jnp.where(kk <= qq, 0.0, MASK_VALUE)

    lam_v = lam_ref[...]
    lam = (jnp.exp(jnp.sum(lam_v[0:1] * lam_v[1:2], axis=-1, keepdims=True))
           - jnp.exp(jnp.sum(lam_v[2:3] * lam_v[3:4], axis=-1, keepdims=True)) + lambda_init)

    def scores(hh, i):
        keys = (i + 1) * tq
        return [jnp.dot(km_ref[hh, mp, :keys, :], qt_ref[hh, i, mp], preferred_element_type=F32)
                for mp in range(2)]

    def attend(bounded):
        work = [(hh, i) for hh in range(HEADS_PER_STEP) for i in range(nq)]
        x_next = scores(*work[0])
        for n, (hh, i) in enumerate(work):
            x_cur = x_next
            if n + 1 < len(work):
                x_next = scores(*work[n + 1])
            keys = (i + 1) * tq
            vt = vt_ref[hh, :, :keys]
            heads_out = []
            for x in x_cur:
                x = (jnp.concatenate([x[:keys - tq], x[keys - tq:] + mask_ref[...]], axis=0) if i
                     else x + mask_ref[...])
                p = jnp.exp2(x) if bounded else jnp.exp2(x - jnp.max(x, axis=0, keepdims=True))
                pv = jnp.dot(vt, p.astype(vt.dtype), preferred_element_type=F32)
                heads_out.append(pv / jnp.sum(p, axis=0, keepdims=True))
            rows = slice(i * tq, (i + 1) * tq)
            hl = slice(hh * DIFF_V_DIM, (hh + 1) * DIFF_V_DIM)
            o = heads_out[0] - lam * heads_out[1]
            o = o * lax.rsqrt(jnp.mean(o * o, axis=0, keepdims=True) + RMS_EPS)
            o = o.T * (subg_ref[...] * (1.0 - lambda_init))
            o_ref[rows, hl] = (o * gate_ref[rows, hl].astype(F32)).astype(o_ref.dtype)

    fast = fast_ref[0] == 1
    pl.when(fast)(functools.partial(attend, True))
    pl.when(jnp.logical_not(fast))(functools.partial(attend, False))


def _diff_attention(acts, blocks, tables, lam_vecs, subg, batch, seq, heads, tq, lambda_init):
    n = acts.shape[0]
    nq = seq // tq
    hps = HEADS_PER_STEP
    q_extra, k_tab, bound_tile, fast = tables
    qb, kb, vb, gb = (b * heads // hps for b in blocks)
    kernel = functools.partial(_diff_attn_kernel, tq=tq, lambda_init=lambda_init)
    head_spec = lambda first: pl.BlockSpec((seq, hps * DIFF_V_DIM), lambda b, h: (b, first + h))
    return pl.pallas_call(
        kernel,
        grid=(batch, heads // hps),
        in_specs=[pl.BlockSpec(memory_space=pltpu.SMEM),
                  pl.BlockSpec((EXTRA_ROWS, V7X_LANES), lambda b, h: (0, 0)),
                  pl.BlockSpec((hps, EXTRA_ROWS, seq), lambda b, h: (h, 0, 0)),
                  pl.BlockSpec((hps, 2, seq, DIFF_V_DIM), lambda b, h: (h, 0, 0, 0)),
                  head_spec(qb), head_spec(kb), head_spec(vb), head_spec(gb),
                  pl.BlockSpec((4, DIFF_HEAD_DIM), lambda b, h: (0, 0)),
                  pl.BlockSpec((1, DIFF_V_DIM), lambda b, h: (0, 0))],
        out_specs=head_spec(0),
        out_shape=jax.ShapeDtypeStruct((n, heads * DIFF_V_DIM), BF16),
        scratch_shapes=[pltpu.VMEM((hps, 2, seq, DIFF_V_DIM), BF16),
                        pltpu.VMEM((hps, DIFF_V_DIM, seq), BF16),
                        pltpu.VMEM((hps, nq, 2, DIFF_V_DIM, tq), BF16),
                        pltpu.VMEM((tq, tq), F32)],
        compiler_params=_cparams(("parallel", "parallel")),
        name="diff_attention",
    )(fast, bound_tile, q_extra, k_tab, acts, acts, acts, acts, lam_vecs, subg.reshape(1, DIFF_V_DIM))


def _xattn_kernel(q_ref, k_ref, v_ref, gate_ref, o_ref, *, heads):
    hd = q_ref.shape[1] // heads
    for h in range(heads):
        sl = slice(h * hd, (h + 1) * hd)
        s = _nt_dot(q_ref[:, sl], k_ref[:, sl])
        m = jnp.max(s, axis=-1, keepdims=True)
        p = jnp.exp(s - m)
        l = jnp.sum(p, axis=-1, keepdims=True)
        o = jnp.dot(p.astype(BF16), v_ref[:, sl], preferred_element_type=F32) / l
        o_ref[:, sl] = (o * gate_ref[:, sl].astype(F32)).astype(o_ref.dtype)


def _cross_attention(acts, q_block, gate_block, mem_kv, batch, seq, mem_len, heads, tq):
    n = acts.shape[0]
    d = mem_kv.shape[1] // 2
    nq = seq // tq
    return pl.pallas_call(
        functools.partial(_xattn_kernel, heads=heads),
        grid=(batch, nq),
        in_specs=[pl.BlockSpec((tq, d), lambda b, i: (b * nq + i, q_block)),
                  pl.BlockSpec((mem_len, d), lambda b, i: (b, 0)),
                  pl.BlockSpec((mem_len, d), lambda b, i: (b, 1)),
                  pl.BlockSpec((tq, d), lambda b, i: (b * nq + i, gate_block))],
        out_specs=pl.BlockSpec((tq, d), lambda b, i: (b * nq + i, 0)),
        out_shape=jax.ShapeDtypeStruct((n, d), BF16),
        compiler_params=_cparams(("parallel", "arbitrary")),
        name="cross_attention",
    )(acts, mem_kv, mem_kv, acts)


def _merge_kernel(x_ref, ca_ref, da_ref, xa_ref, g0_ref, g1_ref, g2_ref, wc_ref, wd_ref, wx_ref, wo_ref, o_ref,
                  wb_ref):
    @pl.when(pl.program_id(0) == 0)
    def _():
        for c, w_ref in enumerate((wc_ref, wd_ref, wx_ref, wo_ref)):
            wb_ref[c] = w_ref[...].astype(wb_ref.dtype)

    y = g0_ref[...].astype(F32) * jnp.dot(ca_ref[...], wb_ref[0], preferred_element_type=F32)
    y = y + g1_ref[...].astype(F32) * jnp.dot(da_ref[...], wb_ref[1], preferred_element_type=F32)
    y = y + g2_ref[...].astype(F32) * jnp.dot(xa_ref[...], wb_ref[2], preferred_element_type=F32)
    o_ref[...] = x_ref[...] + jnp.dot(y.astype(BF16), wb_ref[3], preferred_element_type=F32)


def _merge_out(x2d, ca, da, xa, acts, gate_block0, wc, wd, wx, wo, tm):
    n, d = x2d.shape
    row = lambda i: (i, 0)
    fixed = lambda i: (0, 0)
    act_spec = pl.BlockSpec((tm, d), row)
    w_spec = pl.BlockSpec((d, d), fixed, pipeline_mode=pl.Buffered(1))
    gate_specs = [pl.BlockSpec((tm, d), functools.partial(lambda i, c: (i, c), c=gate_block0 + c))
                  for c in range(N_BRANCH)]
    return pl.pallas_call(
        _merge_kernel,
        grid=(n // tm,),
        in_specs=[pl.BlockSpec((tm, d), row), act_spec, act_spec, act_spec, *gate_specs,
                  w_spec, w_spec, w_spec, w_spec],
        out_specs=pl.BlockSpec((tm, d), row),
        out_shape=jax.ShapeDtypeStruct((n, d), x2d.dtype),
        scratch_shapes=[pltpu.VMEM((4, d, d), BF16)],
        compiler_params=_cparams(("arbitrary",)),
        name="merge_out",
    )(x2d, ca, da, xa, acts, acts, acts, wc, wd, wx, wo)


def _layer(x, mem, l, norm_g, mem_norm_g, w_in, conv_dw, conv_dw_b, conv_ln_g, conv_ln_b, w_conv_proj,
           diff_qn_g, diff_kn_g, lambda_q1, lambda_k1, lambda_q2, lambda_k2, diff_subln_g, w_diff_proj,
           w_mem_kv, x_qn_g, x_kn_g, w_x_proj, w_out):
    batch, seq, d = x.shape
    mem_len = mem.shape[1]
    heads = d // DIFF_V_DIM
    x_head_dim = d // X_HEADS
    assert x_head_dim == V7X_MXU_DIM and d % V7X_MXU_DIM == 0
    n = batch * seq
    x2d = x.reshape(n, d)
    lambda_init = 0.8 - 0.6 * math.exp(-0.3 * l)
    attn_tables = _position_tables(heads, seq) + _score_bound(diff_qn_g, diff_kn_g)

    assert d == PROJ_TN
    g64 = _group_ones(DIFF_HEAD_DIM)
    g256 = _group_ones(V7X_MXU_DIM)
    ones = jnp.ones((d,), F32)

    ops = (("glu", (0, 1)), ("silu", (2,)), ("norm64", (3,)), ("norm64", (4,)), ("none", (5,)), ("silu", (6,)),
           ("norm256", (7,)), ("silu", (8,)), ("sigmoid", (9,)), ("sigmoid", (10,)), ("sigmoid", (11,)))
    U, C_GATE, D_Q, D_K, D_V, D_GATE, X_Q, X_GATE, MERGE = range(9)
    gain_rows = {D_Q: jnp.tile(diff_qn_g, d // DIFF_HEAD_DIM) * (DIFF_HEAD_DIM ** -0.5 * LOG2_E),
                 D_K: jnp.tile(diff_kn_g, d // DIFF_HEAD_DIM),
                 X_Q: jnp.tile(x_qn_g, X_HEADS) * (x_head_dim ** -0.5)}
    gains = jnp.stack([gain_rows.get(b, ones) for b in range(len(ops))]).reshape(len(ops), 1, d)
    acts = _norm_proj(x2d, norm_g, w_in.astype(BF16), ops, gains, g64, g256, tm=256)

    mem_ops = (("norm256", (0,)), ("none", (1,)))
    mem_gains = jnp.stack([jnp.tile(x_kn_g, X_HEADS), ones]).reshape(2, 1, d)
    mem_kv = _norm_proj(mem.reshape(batch * mem_len, d), mem_norm_g, w_mem_kv, mem_ops, mem_gains,
                        g64, g256, tm=256)

    conv_act = _conv_branch(acts, U, C_GATE, conv_dw, conv_dw_b, conv_ln_g, conv_ln_b, batch, seq, ts=256)
    lam_vecs = jnp.stack([lambda_q1, lambda_k1, lambda_q2, lambda_k2])
    diff_act = _diff_attention(acts, (D_Q, D_K, D_V, D_GATE), attn_tables, lam_vecs, diff_subln_g,
                               batch, seq, heads, tq=256, lambda_init=lambda_init)
    x_act = _cross_attention(acts, X_Q, X_GATE, mem_kv, batch, seq, mem_len, X_HEADS, tq=512)

    out = _merge_out(x2d, conv_act, diff_act, x_act, acts, MERGE, w_conv_proj, w_diff_proj, w_x_proj, w_out, tm=512)
    return out.reshape(batch, seq, d)


def kernel(x, mem, norm_g, mem_norm_g, w_in, conv_dw, conv_dw_b, conv_ln_g, conv_ln_b, w_conv_proj, diff_qn_g, diff_kn_g, lambda_q1, lambda_k1, lambda_q2, lambda_k2, diff_subln_g, w_diff_proj, w_mem_kv, x_qn_g, x_kn_g, w_x_proj, w_out):
    params = (norm_g, mem_norm_g, w_in, conv_dw, conv_dw_b, conv_ln_g, conv_ln_b, w_conv_proj, diff_qn_g,
              diff_kn_g, lambda_q1, lambda_k1, lambda_q2, lambda_k2, diff_subln_g, w_diff_proj, w_mem_kv,
              x_qn_g, x_kn_g, w_x_proj, w_out)
    for l in range(norm_g.shape[0]):
        x = _layer(x, mem, l, *(p[l] for p in params))
    return x
```

```python
import functools
import math

import jax
import jax.numpy as jnp
from jax import lax
import numpy as np
from jax.experimental import pallas as pl
from jax.experimental.pallas import tpu as pltpu

CONV_K = 31
DIFF_HEAD_DIM = 64
DIFF_V_DIM = 2 * DIFF_HEAD_DIM
X_HEADS = 4
N_BRANCH = 3
RMS_EPS = 1e-6
LN_EPS = 1e-5
MASK_VALUE = -1e30
LOG2_E = math.log2(math.e)

V7X_LANES = 128
V7X_SUBLANES = 8
V7X_MXU_DIM = 256
V7X_VMEM_LIMIT_BYTES = 56 * 1024 * 1024

BF16 = jnp.bfloat16
F32 = jnp.float32


def _cparams(sem):
    return pltpu.CompilerParams(dimension_semantics=sem, vmem_limit_bytes=V7X_VMEM_LIMIT_BYTES)


def _sigmoid(x):
    return 1.0 / (1.0 + jnp.exp(-x))


def _silu(x):
    return x * _sigmoid(x)


def _nt_dot(a, b):
    return lax.dot_general(a, b, (((1,), (1,)), ((), ())), preferred_element_type=F32)


PROJ_TN = 1024


def _group_rms(x, gmat, group, gain):
    x2 = (x * x).astype(BF16)
    n = x.shape[1]
    ss = jnp.concatenate([jnp.dot(x2[:, c:c + V7X_MXU_DIM], gmat, preferred_element_type=F32)
                          for c in range(0, n, V7X_MXU_DIM)], axis=1)
    return x * lax.rsqrt(ss * (1.0 / group) + RMS_EPS) * gain


def _norm_proj_kernel(x_ref, g_ref, w_ref, gain_ref, g64_ref, g256_ref, o_ref, *wb, ops):
    if wb:
        w_f32, (w_ref,) = w_ref, wb

        @pl.when(pl.program_id(0) == 0)
        def _():
            w_ref[...] = w_f32[...].astype(w_ref.dtype)

    x = x_ref[...]
    h = (x * lax.rsqrt(jnp.mean(x * x, axis=-1, keepdims=True) + RMS_EPS) * g_ref[...]).astype(BF16)

    def proj(blk):
        return jnp.dot(h, w_ref[:, blk * PROJ_TN:(blk + 1) * PROJ_TN], preferred_element_type=F32)

    for ob, (kind, blks) in enumerate(ops):
        if kind == "glu":
            y = proj(blks[0]) * _sigmoid(proj(blks[1]))
        elif kind == "silu":
            y = _silu(proj(blks[0]))
        elif kind == "sigmoid":
            y = _sigmoid(proj(blks[0]))
        elif kind == "norm64":
            y = _group_rms(proj(blks[0]), g64_ref[...], DIFF_HEAD_DIM, gain_ref[ob])
        elif kind == "norm256":
            y = _group_rms(proj(blks[0]), g256_ref[...], V7X_MXU_DIM, gain_ref[ob])
        else:
            assert kind == "none", kind
            y = proj(blks[0])
        o_ref[:, ob * PROJ_TN:(ob + 1) * PROJ_TN] = y.astype(o_ref.dtype)


def _norm_proj(x2d, g, w, ops, gains, g64, g256, tm):
    n, d = x2d.shape
    nout = len(ops) * PROJ_TN

    def resident(a):
        return pl.BlockSpec(a.shape, lambda i: (0,) * a.ndim, pipeline_mode=pl.Buffered(1))

    g2 = g.reshape(1, d)
    cast_in_kernel = w.dtype != BF16
    return pl.pallas_call(
        functools.partial(_norm_proj_kernel, ops=ops),
        grid=(n // tm,),
        in_specs=[pl.BlockSpec((tm, d), lambda i: (i, 0)),
                  resident(g2), resident(w), resident(gains), resident(g64), resident(g256)],
        out_specs=pl.BlockSpec((tm, nout), lambda i: (i, 0)),
        out_shape=jax.ShapeDtypeStruct((n, nout), BF16),
        scratch_shapes=[pltpu.VMEM(w.shape, BF16)] if cast_in_kernel else [],
        compiler_params=_cparams(("arbitrary" if cast_in_kernel else "parallel",)),
        name="norm_proj",
    )(x2d, g2, w, gains, g64, g256)


def _group_ones(group):
    r = jnp.arange(V7X_MXU_DIM) // group
    return (r[:, None] == r[None, :]).astype(BF16)


CONV_HALO = 32
CONV_ROWS = 32
CONV_COLS = 512


def _conv_kernel(u_ref, gate_ref, dw_ref, dwb_ref, lng_ref, lnb_ref, o_ref, win_ref, y_ref, *, ts):
    j = pl.program_id(1)
    t0 = pl.multiple_of(j * ts, ts)
    d = u_ref.shape[1]

    @pl.when(j == 0)
    def _():
        win_ref[0, 0:CONV_HALO, :] = jnp.zeros((CONV_HALO, d), F32)

    @pl.when(j > 0)
    def _():
        win_ref[0, 0:CONV_HALO, :] = u_ref[pl.ds(t0 - CONV_HALO, CONV_HALO), :].astype(F32)

    win_ref[0, CONV_HALO:CONV_HALO + ts, :] = u_ref[pl.ds(t0, ts), :].astype(F32)

    first = CONV_HALO - (CONV_K - 1)
    span = CONV_HALO + ts - V7X_SUBLANES
    for c in range(d // V7X_LANES):
        cs = slice(c * V7X_LANES, (c + 1) * V7X_LANES)
        x = win_ref[0, :, cs]
        for s in range(1, V7X_SUBLANES):
            x = pltpu.roll(x, x.shape[0] - 1, 0)
            win_ref[s, 0:span, cs] = x[0:span]

    for r0 in range(0, ts, CONV_ROWS):
        for c in range(d // CONV_COLS):
            cs = slice(c * CONV_COLS, (c + 1) * CONV_COLS)
            acc = jnp.zeros((CONV_ROWS, CONV_COLS), F32)
            for s in range(V7X_SUBLANES):
                taps = [k for k in range(CONV_K) if (first + k) % V7X_SUBLANES == s]
                lo = r0 + first + taps[0] - s
                big = win_ref[s, lo:lo + (taps[-1] - taps[0]) + CONV_ROWS, cs]
                for k in taps:
                    w = jnp.concatenate([dw_ref[k, :, cs]] * (CONV_ROWS // V7X_SUBLANES), axis=0)
                    acc = acc + big[k - taps[0]:k - taps[0] + CONV_ROWS] * w
            y_ref[r0:r0 + CONV_ROWS, cs] = acc + dwb_ref[:, cs]

    y = y_ref[...]
    mu = jnp.mean(y, axis=-1, keepdims=True)
    yc = y - mu
    yn = yc * lax.rsqrt(jnp.mean(yc * yc, axis=-1, keepdims=True) + LN_EPS)
    yn = yn * lng_ref[...] + lnb_ref[...]
    o_ref[...] = (_silu(yn) * gate_ref[...].astype(F32)).astype(o_ref.dtype)


def _conv_branch(acts, u_block, gate_block, dw, dwb, lng, lnb, batch, seq, ts):
    n, d = acts.shape[0], dw.shape[1]
    nt = seq // ts
    return pl.pallas_call(
        functools.partial(_conv_kernel, ts=ts),
        grid=(batch, nt),
        in_specs=[pl.BlockSpec((seq, d), lambda b, j: (b, u_block)),
                  pl.BlockSpec((ts, d), lambda b, j: (b * nt + j, gate_block)),
                  pl.BlockSpec((CONV_K, V7X_SUBLANES, d), lambda b, j: (0, 0, 0)),
                  pl.BlockSpec((1, d), lambda b, j: (0, 0)),
                  pl.BlockSpec((1, d), lambda b, j: (0, 0)),
                  pl.BlockSpec((1, d), lambda b, j: (0, 0))],
        out_specs=pl.BlockSpec((ts, d), lambda b, j: (b * nt + j, 0)),
        out_shape=jax.ShapeDtypeStruct((n, d), BF16),
        scratch_shapes=[pltpu.VMEM((V7X_SUBLANES, CONV_HALO + ts, d), F32), pltpu.VMEM((ts, d), F32)],
        compiler_params=_cparams(("parallel", "arbitrary")),
        name="conv_branch",
    )(acts, acts, jnp.broadcast_to(dw[:, None, :], (CONV_K, V7X_SUBLANES, d)),
      dwb.reshape(1, d), lng.reshape(1, d), lnb.reshape(1, d))


EXTRA_ROWS = 16
PIECES = 3
POS_RADIX = 128
BF16_NORM_MARGIN = 1.02
MAX_FAST_BOUND = 40.0


def _pieces(v):
    out = []
    for _ in range(PIECES):
        piece = v.astype(BF16).astype(F32)
        out.append(piece)
        v = v - piece
    return out


def _position_tables(heads, seq):
    d = DIFF_HEAD_DIM
    rest = (2.0 ** (-8.0 * np.arange(1, heads + 1, dtype=np.float32) / heads) * np.float32(LOG2_E)).astype(np.float32)
    slopes = []
    for _ in range(PIECES):
        piece = rest.astype(BF16).astype(np.float32)
        slopes.append(piece)
        rest = rest - piece
    pos = np.arange(seq)
    hi = np.broadcast_to((pos // POS_RADIX).astype(np.float32), (heads, seq))
    lo = np.broadcast_to((pos % POS_RADIX).astype(np.float32), (heads, seq))
    const = lambda v: np.broadcast_to(v[:, None], (heads, seq))
    zero = np.zeros((heads, seq), np.float32)
    q_rows = ([const(POS_RADIX * p) for p in slopes] + [const(p) for p in slopes] + [hi] * PIECES + [lo] * PIECES
              + [zero] * (PIECES + 1))
    k_rows = ([hi] * PIECES + [lo] * PIECES + [const(-POS_RADIX * p) for p in slopes] + [const(-p) for p in slopes]
              + [zero + 1.0] * PIECES + [zero])
    q_extra = np.stack(q_rows, axis=1)
    k_extra = np.stack(k_rows, axis=2)
    k_tab = np.zeros((heads, 2, seq, DIFF_V_DIM), np.float32)
    k_tab[:, 0, :, d:d + EXTRA_ROWS] = k_extra
    k_tab[:, 1, :, :EXTRA_ROWS] = k_extra
    return jnp.asarray(q_extra.astype(BF16)), jnp.asarray(k_tab.astype(BF16))


def _score_bound(qn_g, kn_g):
    bound = (DIFF_HEAD_DIM ** 0.5 * LOG2_E * BF16_NORM_MARGIN) * jnp.max(jnp.abs(qn_g)) * jnp.max(jnp.abs(kn_g))
    fast = bound < MAX_FAST_BOUND
    neg_b = _pieces(jnp.where(fast, -bound, 0.0))
    rows = [jnp.zeros((), F32)] * (4 * PIECES) + neg_b + [jnp.zeros((), F32)]
    tile = jnp.broadcast_to(jnp.stack(rows)[:, None], (EXTRA_ROWS, V7X_LANES))
    return tile, fast.astype(jnp.int32).reshape(1)


HEADS_PER_STEP = 2


def _diff_attn_kernel(fast_ref, bound_ref, qx_ref, kx_ref, q_ref, k_ref, v_ref, gate_ref, lam_ref, subg_ref, o_ref,
                      km_ref, vt_ref, qt_ref, mask_ref, *, tq, lambda_init):
    seq = q_ref.shape[0]
    nq = seq // tq
    d = DIFF_HEAD_DIM
    first_half = lax.broadcasted_iota(jnp.int32, (tq, DIFF_V_DIM), 1) < d
    pad = jnp.zeros((DIFF_V_DIM - d - EXTRA_ROWS, tq), F32)
    bound_rows = jnp.concatenate([bound_ref[...]] * (tq // V7X_LANES), axis=1)

    for hh in range(HEADS_PER_STEP):
        hl = slice(hh * DIFF_V_DIM, (hh + 1) * DIFF_V_DIM)
        for c in range(nq):
            rows = slice(c * tq, (c + 1) * tq)
            k = k_ref[rows, hl]
            km_ref[hh, 0, rows, :] = jnp.where(first_half, k, kx_ref[hh, 0, rows, :])
            km_ref[hh, 1, rows, :] = jnp.where(first_half, kx_ref[hh, 1, rows, :], k)
            vt_ref[hh, :, rows] = v_ref[rows, hl].astype(F32).T.astype(vt_ref.dtype)
        for i in range(nq):
            cols = slice(i * tq, (i + 1) * tq)
            extra = qx_ref[hh, :, cols].astype(F32) + bound_rows
            qt = q_ref[cols, hl].astype(F32).T
            qt_ref[hh, i, 0] = jnp.concatenate([qt[:d], extra, pad], axis=0).astype(qt_ref.dtype)
            qt_ref[hh, i, 1] = jnp.concatenate([extra, pad, qt[d:]], axis=0).astype(qt_ref.dtype)

    kk = lax.broadcasted_iota(jnp.int32, (tq, tq), 0)
    qq = lax.broadcasted_iota(jnp.int32, (tq, tq), 1)
    mask_ref[...] = jnp.where(kk <= qq, 0.0, MASK_VALUE)

    lam_v = lam_ref[...]
    lam = (jnp.exp(jnp.sum(lam_v[0:1] * lam_v[1:2], axis=-1, keepdims=True))
           - jnp.exp(jnp.sum(lam_v[2:3] * lam_v[3:4], axis=-1, keepdims=True)) + lambda_init)

    def scores(hh, i):
        keys = (i + 1) * tq
        return [jnp.dot(km_ref[hh, mp, :keys, :], qt_ref[hh, i, mp], preferred_element_type=F32)
                for mp in range(2)]

    def attend(bounded):
        work = [(hh, i) for hh in range(HEADS_PER_STEP) for i in range(nq)]
        x_next = scores(*work[0])
        for n, (hh, i) in enumerate(work):
            x_cur = x_next
            if n + 1 < len(work):
                x_next = scores(*work[n + 1])
            keys = (i + 1) * tq
            vt = vt_ref[hh, :, :keys]
            heads_out = []
            for x in x_cur:
                x = (jnp.concatenate([x[:keys - tq], x[keys - tq:] + mask_ref[...]], axis=0) if i
                     else x + mask_ref[...])
                p = jnp.exp2(x) if bounded else jnp.exp2(x - jnp.max(x, axis=0, keepdims=True))
                pv = jnp.dot(vt, p.astype(vt.dtype), preferred_element_type=F32)
                heads_out.append(pv / jnp.sum(p, axis=0, keepdims=True))
            rows = slice(i * tq, (i + 1) * tq)
            hl = slice(hh * DIFF_V_DIM, (hh + 1) * DIFF_V_DIM)
            o = heads_out[0] - lam * heads_out[1]
            o = o * lax.rsqrt(jnp.mean(o * o, axis=0, keepdims=True) + RMS_EPS)
            o = o.T * (subg_ref[...] * (1.0 - lambda_init))
            o_ref[rows, hl] = (o * gate_ref[rows, hl].astype(F32)).astype(o_ref.dtype)

    fast = fast_ref[0] == 1
    pl.when(fast)(functools.partial(attend, True))
    pl.when(jnp.logical_not(fast))(functools.partial(attend, False))


def _diff_attention(acts, blocks, tables, lam_vecs, subg, batch, seq, heads, tq, lambda_init):
    n = acts.shape[0]
    nq = seq // tq
    hps = HEADS_PER_STEP
    q_extra, k_tab, bound_tile, fast = tables
    qb, kb, vb, gb = (b * heads // hps for b in blocks)
    kernel = functools.partial(_diff_attn_kernel, tq=tq, lambda_init=lambda_init)
    head_spec = lambda first: pl.BlockSpec((seq, hps * DIFF_V_DIM), lambda b, h: (b, first + h))
    return pl.pallas_call(
        kernel,
        grid=(batch, heads // hps),
        in_specs=[pl.BlockSpec(memory_space=pltpu.SMEM),
                  pl.BlockSpec((EXTRA_ROWS, V7X_LANES), lambda b, h: (0, 0)),
                  pl.BlockSpec((hps, EXTRA_ROWS, seq), lambda b, h: (h, 0, 0)),
                  pl.BlockSpec((hps, 2, seq, DIFF_V_DIM), lambda b, h: (h, 0, 0, 0)),
                  head_spec(qb), head_spec(kb), head_spec(vb), head_spec(gb),
                  pl.BlockSpec((4, DIFF_HEAD_DIM), lambda b, h: (0, 0)),
                  pl.BlockSpec((1, DIFF_V_DIM), lambda b, h: (0, 0))],
        out_specs=head_spec(0),
        out_shape=jax.ShapeDtypeStruct((n, heads * DIFF_V_DIM), BF16),
        scratch_shapes=[pltpu.VMEM((hps, 2, seq, DIFF_V_DIM), BF16),
                        pltpu.VMEM((hps, DIFF_V_DIM, seq), BF16),
                        pltpu.VMEM((hps, nq, 2, DIFF_V_DIM, tq), BF16),
                        pltpu.VMEM((tq, tq), F32)],
        compiler_params=_cparams(("parallel", "parallel")),
        name="diff_attention",
    )(fast, bound_tile, q_extra, k_tab, acts, acts, acts, acts, lam_vecs, subg.reshape(1, DIFF_V_DIM))


def _cross_attend(q_ref, k_ref, v_ref, gate_ref, heads):
    hd = q_ref.shape[1] // heads
    outs = []
    for h in range(heads):
        sl = slice(h * hd, (h + 1) * hd)
        s = _nt_dot(q_ref[:, sl], k_ref[:, sl])
        m = jnp.max(s, axis=-1, keepdims=True)
        p = jnp.exp(s - m)
        l = jnp.sum(p, axis=-1, keepdims=True)
        o = jnp.dot(p.astype(BF16), v_ref[:, sl], preferred_element_type=F32) / l
        outs.append(o * gate_ref[:, sl].astype(F32))
    return jnp.concatenate(outs, axis=1)


def _merge_kernel(x_ref, ca_ref, da_ref, xq_ref, xk_ref, xv_ref, xg_ref, g0_ref, g1_ref, g2_ref,
                  wc_ref, wd_ref, wx_ref, wo_ref, o_ref, wb_ref, *, x_heads):
    @pl.when(pl.program_id(0) == 0)
    def _():
        for c, w_ref in enumerate((wc_ref, wd_ref, wx_ref, wo_ref)):
            wb_ref[c] = w_ref[...].astype(wb_ref.dtype)

    xa = _cross_attend(xq_ref, xk_ref, xv_ref, xg_ref, x_heads).astype(BF16)
    y = g0_ref[...].astype(F32) * jnp.dot(ca_ref[...], wb_ref[0], preferred_element_type=F32)
    y = y + g1_ref[...].astype(F32) * jnp.dot(da_ref[...], wb_ref[1], preferred_element_type=F32)
    y = y + g2_ref[...].astype(F32) * jnp.dot(xa, wb_ref[2], preferred_element_type=F32)
    o_ref[...] = x_ref[...] + jnp.dot(y.astype(BF16), wb_ref[3], preferred_element_type=F32)


def _merge_out(x2d, ca, da, acts, xq_block, xgate_block, gate_block0, mem_kv, wc, wd, wx, wo, seq, mem_len, tm):
    n, d = x2d.shape
    tiles_per_seq = seq // tm
    row = lambda i: (i, 0)
    fixed = lambda i: (0, 0)
    act_spec = pl.BlockSpec((tm, d), row)
    acts_spec = lambda blk: pl.BlockSpec((tm, d), lambda i: (i, blk))
    mem_spec = lambda half: pl.BlockSpec((mem_len, d), lambda i: (i // tiles_per_seq, half))
    w_spec = pl.BlockSpec((d, d), fixed, pipeline_mode=pl.Buffered(1))
    return pl.pallas_call(
        functools.partial(_merge_kernel, x_heads=X_HEADS),
        grid=(n // tm,),
        in_specs=[pl.BlockSpec((tm, d), row), act_spec, act_spec,
                  acts_spec(xq_block), mem_spec(0), mem_spec(1), acts_spec(xgate_block),
                  *[acts_spec(gate_block0 + c) for c in range(N_BRANCH)],
                  w_spec, w_spec, w_spec, w_spec],
        out_specs=pl.BlockSpec((tm, d), row),
        out_shape=jax.ShapeDtypeStruct((n, d), x2d.dtype),
        scratch_shapes=[pltpu.VMEM((4, d, d), BF16)],
        compiler_params=_cparams(("arbitrary",)),
        name="merge_out",
    )(x2d, ca, da, acts, mem_kv, mem_kv, acts, acts, acts, acts, wc, wd, wx, wo)


def _layer(x, mem, l, norm_g, mem_norm_g, w_in, conv_dw, conv_dw_b, conv_ln_g, conv_ln_b, w_conv_proj,
           diff_qn_g, diff_kn_g, lambda_q1, lambda_k1, lambda_q2, lambda_k2, diff_subln_g, w_diff_proj,
           w_mem_kv, x_qn_g, x_kn_g, w_x_proj, w_out):
    batch, seq, d = x.shape
    mem_len = mem.shape[1]
    heads = d // DIFF_V_DIM
    x_head_dim = d // X_HEADS
    assert x_head_dim == V7X_MXU_DIM and d % V7X_MXU_DIM == 0
    n = batch * seq
    x2d = x.reshape(n, d)
    lambda_init = 0.8 - 0.6 * math.exp(-0.3 * l)
    attn_tables = _position_tables(heads, seq) + _score_bound(diff_qn_g, diff_kn_g)

    assert d == PROJ_TN
    g64 = _group_ones(DIFF_HEAD_DIM)
    g256 = _group_ones(V7X_MXU_DIM)
    ones = jnp.ones((d,), F32)

    ops = (("glu", (0, 1)), ("silu", (2,)), ("norm64", (3,)), ("norm64", (4,)), ("none", (5,)), ("silu", (6,)),
           ("norm256", (7,)), ("silu", (8,)), ("sigmoid", (9,)), ("sigmoid", (10,)), ("sigmoid", (11,)))
    U, C_GATE, D_Q, D_K, D_V, D_GATE, X_Q, X_GATE, MERGE = range(9)
    gain_rows = {D_Q: jnp.tile(diff_qn_g, d // DIFF_HEAD_DIM) * (DIFF_HEAD_DIM ** -0.5 * LOG2_E),
                 D_K: jnp.tile(diff_kn_g, d // DIFF_HEAD_DIM),
                 X_Q: jnp.tile(x_qn_g, X_HEADS) * (x_head_dim ** -0.5)}
    gains = jnp.stack([gain_rows.get(b, ones) for b in range(len(ops))]).reshape(len(ops), 1, d)
    acts = _norm_proj(x2d, norm_g, w_in.astype(BF16), ops, gains, g64, g256, tm=256)

    mem_ops = (("norm256", (0,)), ("none", (1,)))
    mem_gains = jnp.stack([jnp.tile(x_kn_g, X_HEADS), ones]).reshape(2, 1, d)
    mem_kv = _norm_proj(mem.reshape(batch * mem_len, d), mem_norm_g, w_mem_kv, mem_ops, mem_gains,
                        g64, g256, tm=256)

    conv_act = _conv_branch(acts, U, C_GATE, conv_dw, conv_dw_b, conv_ln_g, conv_ln_b, batch, seq, ts=256)
    lam_vecs = jnp.stack([lambda_q1, lambda_k1, lambda_q2, lambda_k2])
    diff_act = _diff_attention(acts, (D_Q, D_K, D_V, D_GATE), attn_tables, lam_vecs, diff_subln_g,
                               batch, seq, heads, tq=256, lambda_init=lambda_init)
    out = _merge_out(x2d, conv_act, diff_act, acts, X_Q, X_GATE, MERGE, mem_kv,
                     w_conv_proj, w_diff_proj, w_x_proj, w_out, seq, mem_len, tm=512)
    return out.reshape(batch, seq, d)


def kernel(x, mem, norm_g, mem_norm_g, w_in, conv_dw, conv_dw_b, conv_ln_g, conv_ln_b, w_conv_proj, diff_qn_g, diff_kn_g, lambda_q1, lambda_k1, lambda_q2, lambda_k2, diff_subln_g, w_diff_proj, w_mem_kv, x_qn_g, x_kn_g, w_x_proj, w_out):
    params = (norm_g, mem_norm_g, w_in, conv_dw, conv_dw_b, conv_ln_g, conv_ln_b, w_conv_proj, diff_qn_g,
              diff_kn_g, lambda_q1, lambda_k1, lambda_q2, lambda_k2, diff_subln_g, w_diff_proj, w_mem_kv,
              x_qn_g, x_kn_g, w_x_proj, w_out)
    for l in range(norm_g.shape[0]):
        x = _layer(x, mem, l, *(p[l] for p in params))
    return x
```

```python
import functools
import math

import jax
import jax.numpy as jnp
from jax import lax
import numpy as np
from jax.experimental import pallas as pl
from jax.experimental.pallas import tpu as pltpu

CONV_K = 31
DIFF_HEAD_DIM = 64
DIFF_V_DIM = 2 * DIFF_HEAD_DIM
X_HEADS = 4
N_BRANCH = 3
RMS_EPS = 1e-6
LN_EPS = 1e-5
MASK_VALUE = -1e30
LOG2_E = math.log2(math.e)

V7X_LANES = 128
V7X_SUBLANES = 8
V7X_MXU_DIM = 256
V7X_VMEM_LIMIT_BYTES = 56 * 1024 * 1024

BF16 = jnp.bfloat16
F32 = jnp.float32


def _cparams(sem):
    return pltpu.CompilerParams(dimension_semantics=sem, vmem_limit_bytes=V7X_VMEM_LIMIT_BYTES)


def _sigmoid(x):
    return 1.0 / (1.0 + jnp.exp(-x))


def _silu(x):
    return x * _sigmoid(x)


def _nt_dot(a, b):
    return lax.dot_general(a, b, (((1,), (1,)), ((), ())), preferred_element_type=F32)


PROJ_TN = 1024


def _group_rms(x, gmat, group, gain):
    x2 = (x * x).astype(BF16)
    n = x.shape[1]
    ss = jnp.concatenate([jnp.dot(x2[:, c:c + V7X_MXU_DIM], gmat, preferred_element_type=F32)
                          for c in range(0, n, V7X_MXU_DIM)], axis=1)
    return x * lax.rsqrt(ss * (1.0 / group) + RMS_EPS) * gain


def _norm_proj_kernel(x_ref, g_ref, w_ref, gain_ref, g64_ref, g256_ref, o_ref, *rest, ops):
    n_t = sum(dest is not None for _, _, dest in ops)
    t_refs, wb = rest[:n_t], rest[n_t:]
    if wb:
        w_f32, (w_ref,) = w_ref, wb

        @pl.when(pl.program_id(0) == 0)
        def _():
            w_ref[...] = w_f32[...].astype(w_ref.dtype)

    x = x_ref[...]
    h = (x * lax.rsqrt(jnp.mean(x * x, axis=-1, keepdims=True) + RMS_EPS) * g_ref[...]).astype(BF16)

    def proj(blk):
        return jnp.dot(h, w_ref[:, blk * PROJ_TN:(blk + 1) * PROJ_TN], preferred_element_type=F32)

    ob = 0
    for n, (kind, blks, dest) in enumerate(ops):
        if kind == "glu":
            y = proj(blks[0]) * _sigmoid(proj(blks[1]))
        elif kind == "silu":
            y = _silu(proj(blks[0]))
        elif kind == "sigmoid":
            y = _sigmoid(proj(blks[0]))
        elif kind == "norm64":
            y = _group_rms(proj(blks[0]), g64_ref[...], DIFF_HEAD_DIM, gain_ref[n])
        elif kind == "norm256":
            y = _group_rms(proj(blks[0]), g256_ref[...], V7X_MXU_DIM, gain_ref[n])
        else:
            assert kind == "none", kind
            y = proj(blks[0])
        if dest is None:
            o_ref[:, ob * PROJ_TN:(ob + 1) * PROJ_TN] = y.astype(o_ref.dtype)
            ob += 1
        else:
            t_refs[dest][...] = y.T.astype(t_refs[dest].dtype)


def _norm_proj(x2d, g, w, ops, gains, g64, g256, tm):
    n, d = x2d.shape
    n_t = sum(dest is not None for _, _, dest in ops)
    nout = (len(ops) - n_t) * PROJ_TN

    def resident(a):
        return pl.BlockSpec(a.shape, lambda i: (0,) * a.ndim, pipeline_mode=pl.Buffered(1))

    g2 = g.reshape(1, d)
    cast_in_kernel = w.dtype != BF16
    return pl.pallas_call(
        functools.partial(_norm_proj_kernel, ops=ops),
        grid=(n // tm,),
        in_specs=[pl.BlockSpec((tm, d), lambda i: (i, 0)),
                  resident(g2), resident(w), resident(gains), resident(g64), resident(g256)],
        out_specs=[pl.BlockSpec((tm, nout), lambda i: (i, 0))]
                  + [pl.BlockSpec((PROJ_TN, tm), lambda i: (0, i))] * n_t,
        out_shape=[jax.ShapeDtypeStruct((n, nout), BF16)] + [jax.ShapeDtypeStruct((PROJ_TN, n), BF16)] * n_t,
        scratch_shapes=[pltpu.VMEM(w.shape, BF16)] if cast_in_kernel else [],
        compiler_params=_cparams(("arbitrary" if cast_in_kernel else "parallel",)),
        name="norm_proj",
    )(x2d, g2, w, gains, g64, g256)


def _group_ones(group):
    r = jnp.arange(V7X_MXU_DIM) // group
    return (r[:, None] == r[None, :]).astype(BF16)


CONV_HALO = 32
CONV_ROWS = 32
CONV_COLS = 512


def _conv_kernel(u_ref, gate_ref, dw_ref, dwb_ref, lng_ref, lnb_ref, o_ref, win_ref, y_ref, *, ts):
    j = pl.program_id(1)
    t0 = pl.multiple_of(j * ts, ts)
    d = u_ref.shape[1]

    @pl.when(j == 0)
    def _():
        win_ref[0, 0:CONV_HALO, :] = jnp.zeros((CONV_HALO, d), F32)

    @pl.when(j > 0)
    def _():
        win_ref[0, 0:CONV_HALO, :] = u_ref[pl.ds(t0 - CONV_HALO, CONV_HALO), :].astype(F32)

    win_ref[0, CONV_HALO:CONV_HALO + ts, :] = u_ref[pl.ds(t0, ts), :].astype(F32)

    first = CONV_HALO - (CONV_K - 1)
    span = CONV_HALO + ts - V7X_SUBLANES
    for c in range(d // V7X_LANES):
        cs = slice(c * V7X_LANES, (c + 1) * V7X_LANES)
        x = win_ref[0, :, cs]
        for s in range(1, V7X_SUBLANES):
            x = pltpu.roll(x, x.shape[0] - 1, 0)
            win_ref[s, 0:span, cs] = x[0:span]

    for r0 in range(0, ts, CONV_ROWS):
        for c in range(d // CONV_COLS):
            cs = slice(c * CONV_COLS, (c + 1) * CONV_COLS)
            acc = jnp.zeros((CONV_ROWS, CONV_COLS), F32)
            for s in range(V7X_SUBLANES):
                taps = [k for k in range(CONV_K) if (first + k) % V7X_SUBLANES == s]
                lo = r0 + first + taps[0] - s
                big = win_ref[s, lo:lo + (taps[-1] - taps[0]) + CONV_ROWS, cs]
                for k in taps:
                    w = jnp.concatenate([dw_ref[k, :, cs]] * (CONV_ROWS // V7X_SUBLANES), axis=0)
                    acc = acc + big[k - taps[0]:k - taps[0] + CONV_ROWS] * w
            y_ref[r0:r0 + CONV_ROWS, cs] = acc + dwb_ref[:, cs]

    y = y_ref[...]
    mu = jnp.mean(y, axis=-1, keepdims=True)
    yc = y - mu
    yn = yc * lax.rsqrt(jnp.mean(yc * yc, axis=-1, keepdims=True) + LN_EPS)
    yn = yn * lng_ref[...] + lnb_ref[...]
    o_ref[...] = (_silu(yn) * gate_ref[...].astype(F32)).astype(o_ref.dtype)


def _conv_branch(acts, u_block, gate_block, dw, dwb, lng, lnb, batch, seq, ts):
    n, d = acts.shape[0], dw.shape[1]
    nt = seq // ts
    return pl.pallas_call(
        functools.partial(_conv_kernel, ts=ts),
        grid=(batch, nt),
        in_specs=[pl.BlockSpec((seq, d), lambda b, j: (b, u_block)),
                  pl.BlockSpec((ts, d), lambda b, j: (b * nt + j, gate_block)),
                  pl.BlockSpec((CONV_K, V7X_SUBLANES, d), lambda b, j: (0, 0, 0)),
                  pl.BlockSpec((1, d), lambda b, j: (0, 0)),
                  pl.BlockSpec((1, d), lambda b, j: (0, 0)),
                  pl.BlockSpec((1, d), lambda b, j: (0, 0))],
        out_specs=pl.BlockSpec((ts, d), lambda b, j: (b * nt + j, 0)),
        out_shape=jax.ShapeDtypeStruct((n, d), BF16),
        scratch_shapes=[pltpu.VMEM((V7X_SUBLANES, CONV_HALO + ts, d), F32), pltpu.VMEM((ts, d), F32)],
        compiler_params=_cparams(("parallel", "arbitrary")),
        name="conv_branch",
    )(acts, acts, jnp.broadcast_to(dw[:, None, :], (CONV_K, V7X_SUBLANES, d)),
      dwb.reshape(1, d), lng.reshape(1, d), lnb.reshape(1, d))


EXTRA_ROWS = 16
PIECES = 3
POS_RADIX = 128
BF16_NORM_MARGIN = 1.02
MAX_FAST_BOUND = 40.0


def _pieces(v):
    out = []
    for _ in range(PIECES):
        piece = v.astype(BF16).astype(F32)
        out.append(piece)
        v = v - piece
    return out


def _position_tables(heads, seq):
    d = DIFF_HEAD_DIM
    rest = (2.0 ** (-8.0 * np.arange(1, heads + 1, dtype=np.float32) / heads) * np.float32(LOG2_E)).astype(np.float32)
    slopes = []
    for _ in range(PIECES):
        piece = rest.astype(BF16).astype(np.float32)
        slopes.append(piece)
        rest = rest - piece
    pos = np.arange(seq)
    hi = np.broadcast_to((pos // POS_RADIX).astype(np.float32), (heads, seq))
    lo = np.broadcast_to((pos % POS_RADIX).astype(np.float32), (heads, seq))
    const = lambda v: np.broadcast_to(v[:, None], (heads, seq))
    zero = np.zeros((heads, seq), np.float32)
    q_rows = ([const(POS_RADIX * p) for p in slopes] + [const(p) for p in slopes] + [hi] * PIECES + [lo] * PIECES
              + [zero] * (PIECES + 1))
    k_rows = ([hi] * PIECES + [lo] * PIECES + [const(-POS_RADIX * p) for p in slopes] + [const(-p) for p in slopes]
              + [zero + 1.0] * PIECES + [zero])
    q_extra = np.stack(q_rows, axis=1)
    k_extra = np.stack(k_rows, axis=2)
    k_tab = np.zeros((heads, 2, seq, DIFF_V_DIM), np.float32)
    k_tab[:, 0, :, d:d + EXTRA_ROWS] = k_extra
    k_tab[:, 1, :, :EXTRA_ROWS] = k_extra
    return jnp.asarray(q_extra.astype(BF16)), jnp.asarray(k_tab.astype(BF16))


def _score_bound(qn_g, kn_g):
    bound = (DIFF_HEAD_DIM ** 0.5 * LOG2_E * BF16_NORM_MARGIN) * jnp.max(jnp.abs(qn_g)) * jnp.max(jnp.abs(kn_g))
    fast = bound < MAX_FAST_BOUND
    neg_b = _pieces(jnp.where(fast, -bound, 0.0))
    rows = [jnp.zeros((), F32)] * (4 * PIECES) + neg_b + [jnp.zeros((), F32)]
    tile = jnp.broadcast_to(jnp.stack(rows)[:, None], (EXTRA_ROWS, V7X_LANES))
    return tile, fast.astype(jnp.int32).reshape(1)


HEADS_PER_STEP = 2


def _diff_attn_kernel(fast_ref, bound_ref, qx_ref, kx_ref, qt_in_ref, k_ref, vt_ref, gate_ref, lam_ref, subg_ref,
                      o_ref, km_ref, qt_ref, mask_ref, *, tq, lambda_init):
    seq = k_ref.shape[0]
    nq = seq // tq
    d = DIFF_HEAD_DIM
    first_half = lax.broadcasted_iota(jnp.int32, (tq, DIFF_V_DIM), 1) < d
    pad = jnp.zeros((DIFF_V_DIM - d - EXTRA_ROWS, tq), qt_ref.dtype)
    bound_rows = jnp.concatenate([bound_ref[...]] * (tq // V7X_LANES), axis=1)

    for hh in range(HEADS_PER_STEP):
        hl = slice(hh * DIFF_V_DIM, (hh + 1) * DIFF_V_DIM)
        for c in range(nq):
            rows = slice(c * tq, (c + 1) * tq)
            k = k_ref[rows, hl]
            km_ref[hh, 0, rows, :] = jnp.where(first_half, k, kx_ref[hh, 0, rows, :])
            km_ref[hh, 1, rows, :] = jnp.where(first_half, kx_ref[hh, 1, rows, :], k)
        for i in range(nq):
            cols = slice(i * tq, (i + 1) * tq)
            extra = (qx_ref[hh, :, cols].astype(F32) + bound_rows).astype(qt_ref.dtype)
            qt_ref[hh, i, 0] = jnp.concatenate([qt_in_ref[hh * DIFF_V_DIM:hh * DIFF_V_DIM + d, cols], extra, pad], axis=0)
            qt_ref[hh, i, 1] = jnp.concatenate([extra, pad, qt_in_ref[hh * DIFF_V_DIM + d:(hh + 1) * DIFF_V_DIM, cols]],
                                               axis=0)

    kk = lax.broadcasted_iota(jnp.int32, (tq, tq), 0)
    qq = lax.broadcasted_iota(jnp.int32, (tq, tq), 1)
    mask_ref[...] = jnp.where(kk <= qq, 0.0, MASK_VALUE)

    lam_v = lam_ref[...]
    lam = (jnp.exp(jnp.sum(lam_v[0:1] * lam_v[1:2], axis=-1, keepdims=True))
           - jnp.exp(jnp.sum(lam_v[2:3] * lam_v[3:4], axis=-1, keepdims=True)) + lambda_init)

    def scores(hh, i):
        keys = (i + 1) * tq
        return [jnp.dot(km_ref[hh, mp, :keys, :], qt_ref[hh, i, mp], preferred_element_type=F32)
                for mp in range(2)]

    def attend(bounded):
        work = [(hh, i) for hh in range(HEADS_PER_STEP) for i in range(nq)]
        x_next = scores(*work[0])
        for n, (hh, i) in enumerate(work):
            x_cur = x_next
            if n + 1 < len(work):
                x_next = scores(*work[n + 1])
            keys = (i + 1) * tq
            vt = vt_ref[hh * DIFF_V_DIM:(hh + 1) * DIFF_V_DIM, :keys]
            heads_out = []
            for x in x_cur:
                x = (jnp.concatenate([x[:keys - tq], x[keys - tq:] + mask_ref[...]], axis=0) if i
                     else x + mask_ref[...])
                p = jnp.exp2(x) if bounded else jnp.exp2(x - jnp.max(x, axis=0, keepdims=True))
                pv = jnp.dot(vt, p.astype(vt.dtype), preferred_element_type=F32)
                heads_out.append(pv / jnp.sum(p, axis=0, keepdims=True))
            rows = slice(i * tq, (i + 1) * tq)
            hl = slice(hh * DIFF_V_DIM, (hh + 1) * DIFF_V_DIM)
            o = heads_out[0] - lam * heads_out[1]
            o = o * lax.rsqrt(jnp.mean(o * o, axis=0, keepdims=True) + RMS_EPS)
            o = o.T * (subg_ref[...] * (1.0 - lambda_init))
            o_ref[rows, hl] = (o * gate_ref[rows, hl].astype(F32)).astype(o_ref.dtype)

    fast = fast_ref[0] == 1
    pl.when(fast)(functools.partial(attend, True))
    pl.when(jnp.logical_not(fast))(functools.partial(attend, False))


def _diff_attention(acts, q_t, v_t, k_block, gate_block, tables, lam_vecs, subg, batch, seq, heads, tq, lambda_init):
    n = acts.shape[0]
    nq = seq // tq
    hps = HEADS_PER_STEP
    q_extra, k_tab, bound_tile, fast = tables
    kb, gb = (b * heads // hps for b in (k_block, gate_block))
    kernel = functools.partial(_diff_attn_kernel, tq=tq, lambda_init=lambda_init)
    head_spec = lambda first: pl.BlockSpec((seq, hps * DIFF_V_DIM), lambda b, h: (b, first + h))
    t_spec = pl.BlockSpec((hps * DIFF_V_DIM, seq), lambda b, h: (h, b))
    return pl.pallas_call(
        kernel,
        grid=(batch, heads // hps),
        in_specs=[pl.BlockSpec(memory_space=pltpu.SMEM),
                  pl.BlockSpec((EXTRA_ROWS, V7X_LANES), lambda b, h: (0, 0)),
                  pl.BlockSpec((hps, EXTRA_ROWS, seq), lambda b, h: (h, 0, 0)),
                  pl.BlockSpec((hps, 2, seq, DIFF_V_DIM), lambda b, h: (h, 0, 0, 0)),
                  t_spec, head_spec(kb), t_spec, head_spec(gb),
                  pl.BlockSpec((4, DIFF_HEAD_DIM), lambda b, h: (0, 0)),
                  pl.BlockSpec((1, DIFF_V_DIM), lambda b, h: (0, 0))],
        out_specs=head_spec(0),
        out_shape=jax.ShapeDtypeStruct((n, heads * DIFF_V_DIM), BF16),
        scratch_shapes=[pltpu.VMEM((hps, 2, seq, DIFF_V_DIM), BF16),
                        pltpu.VMEM((hps, nq, 2, DIFF_V_DIM, tq), BF16),
                        pltpu.VMEM((tq, tq), F32)],
        compiler_params=_cparams(("parallel", "parallel")),
        name="diff_attention",
    )(fast, bound_tile, q_extra, k_tab, q_t, acts, v_t, acts, lam_vecs, subg.reshape(1, DIFF_V_DIM))


def _cross_attend(q_ref, k_ref, v_ref, gate_ref, heads):
    hd = q_ref.shape[1] // heads
    outs = []
    for h in range(heads):
        sl = slice(h * hd, (h + 1) * hd)
        s = _nt_dot(q_ref[:, sl], k_ref[:, sl])
        m = jnp.max(s, axis=-1, keepdims=True)
        p = jnp.exp(s - m)
        l = jnp.sum(p, axis=-1, keepdims=True)
        o = jnp.dot(p.astype(BF16), v_ref[:, sl], preferred_element_type=F32) / l
        outs.append(o * gate_ref[:, sl].astype(F32))
    return jnp.concatenate(outs, axis=1)


def _merge_kernel(x_ref, ca_ref, da_ref, xq_ref, xk_ref, xv_ref, xg_ref, g0_ref, g1_ref, g2_ref,
                  wc_ref, wd_ref, wx_ref, wo_ref, o_ref, wb_ref, *, x_heads):
    @pl.when(pl.program_id(0) == 0)
    def _():
        for c, w_ref in enumerate((wc_ref, wd_ref, wx_ref, wo_ref)):
            wb_ref[c] = w_ref[...].astype(wb_ref.dtype)

    xa = _cross_attend(xq_ref, xk_ref, xv_ref, xg_ref, x_heads).astype(BF16)
    y = g0_ref[...].astype(F32) * jnp.dot(ca_ref[...], wb_ref[0], preferred_element_type=F32)
    y = y + g1_ref[...].astype(F32) * jnp.dot(da_ref[...], wb_ref[1], preferred_element_type=F32)
    y = y + g2_ref[...].astype(F32) * jnp.dot(xa, wb_ref[2], preferred_element_type=F32)
    o_ref[...] = x_ref[...] + jnp.dot(y.astype(BF16), wb_ref[3], preferred_element_type=F32)


def _merge_out(x2d, ca, da, acts, xq_block, xgate_block, gate_block0, mem_kv, wc, wd, wx, wo, seq, mem_len, tm):
    n, d = x2d.shape
    tiles_per_seq = seq // tm
    row = lambda i: (i, 0)
    fixed = lambda i: (0, 0)
    act_spec = pl.BlockSpec((tm, d), row)
    acts_spec = lambda blk: pl.BlockSpec((tm, d), lambda i: (i, blk))
    mem_spec = lambda half: pl.BlockSpec((mem_len, d), lambda i: (i // tiles_per_seq, half))
    w_spec = pl.BlockSpec((d, d), fixed, pipeline_mode=pl.Buffered(1))
    return pl.pallas_call(
        functools.partial(_merge_kernel, x_heads=X_HEADS),
        grid=(n // tm,),
        in_specs=[pl.BlockSpec((tm, d), row), act_spec, act_spec,
                  acts_spec(xq_block), mem_spec(0), mem_spec(1), acts_spec(xgate_block),
                  *[acts_spec(gate_block0 + c) for c in range(N_BRANCH)],
                  w_spec, w_spec, w_spec, w_spec],
        out_specs=pl.BlockSpec((tm, d), row),
        out_shape=jax.ShapeDtypeStruct((n, d), x2d.dtype),
        scratch_shapes=[pltpu.VMEM((4, d, d), BF16)],
        compiler_params=_cparams(("arbitrary",)),
        name="merge_out",
    )(x2d, ca, da, acts, mem_kv, mem_kv, acts, acts, acts, acts, wc, wd, wx, wo)


def _layer(x, mem, l, norm_g, mem_norm_g, w_in, conv_dw, conv_dw_b, conv_ln_g, conv_ln_b, w_conv_proj,
           diff_qn_g, diff_kn_g, lambda_q1, lambda_k1, lambda_q2, lambda_k2, diff_subln_g, w_diff_proj,
           w_mem_kv, x_qn_g, x_kn_g, w_x_proj, w_out):
    batch, seq, d = x.shape
    mem_len = mem.shape[1]
    heads = d // DIFF_V_DIM
    x_head_dim = d // X_HEADS
    assert x_head_dim == V7X_MXU_DIM and d % V7X_MXU_DIM == 0
    n = batch * seq
    x2d = x.reshape(n, d)
    lambda_init = 0.8 - 0.6 * math.exp(-0.3 * l)
    attn_tables = _position_tables(heads, seq) + _score_bound(diff_qn_g, diff_kn_g)

    assert d == PROJ_TN
    g64 = _group_ones(DIFF_HEAD_DIM)
    g256 = _group_ones(V7X_MXU_DIM)
    ones = jnp.ones((d,), F32)

    ops = (("glu", (0, 1), None), ("silu", (2,), None), ("norm64", (3,), 0), ("norm64", (4,), None),
           ("none", (5,), 1), ("silu", (6,), None), ("norm256", (7,), None), ("silu", (8,), None),
           ("sigmoid", (9,), None), ("sigmoid", (10,), None), ("sigmoid", (11,), None))
    U, C_GATE, D_K, D_GATE, X_Q, X_GATE, MERGE = range(7)
    OP_D_Q, OP_D_K, OP_X_Q = 2, 3, 6
    gain_rows = {OP_D_Q: jnp.tile(diff_qn_g, d // DIFF_HEAD_DIM) * (DIFF_HEAD_DIM ** -0.5 * LOG2_E),
                 OP_D_K: jnp.tile(diff_kn_g, d // DIFF_HEAD_DIM),
                 OP_X_Q: jnp.tile(x_qn_g, X_HEADS) * (x_head_dim ** -0.5)}
    gains = jnp.stack([gain_rows.get(b, ones) for b in range(len(ops))]).reshape(len(ops), 1, d)
    acts, q_t, v_t = _norm_proj(x2d, norm_g, w_in.astype(BF16), ops, gains, g64, g256, tm=256)

    mem_ops = (("norm256", (0,), None), ("none", (1,), None))
    mem_gains = jnp.stack([jnp.tile(x_kn_g, X_HEADS), ones]).reshape(2, 1, d)
    mem_kv, = _norm_proj(mem.reshape(batch * mem_len, d), mem_norm_g, w_mem_kv, mem_ops, mem_gains,
                         g64, g256, tm=256)

    conv_act = _conv_branch(acts, U, C_GATE, conv_dw, conv_dw_b, conv_ln_g, conv_ln_b, batch, seq, ts=256)
    lam_vecs = jnp.stack([lambda_q1, lambda_k1, lambda_q2, lambda_k2])
    diff_act = _diff_attention(acts, q_t, v_t, D_K, D_GATE, attn_tables, lam_vecs, diff_subln_g,
                               batch, seq, heads, tq=256, lambda_init=lambda_init)
    out = _merge_out(x2d, conv_act, diff_act, acts, X_Q, X_GATE, MERGE, mem_kv,
                     w_conv_proj, w_diff_proj, w_x_proj, w_out, seq, mem_len, tm=512)
    return out.reshape(batch, seq, d)


def kernel(x, mem, norm_g, mem_norm_g, w_in, conv_dw, conv_dw_b, conv_ln_g, conv_ln_b, w_conv_proj, diff_qn_g, diff_kn_g, lambda_q1, lambda_k1, lambda_q2, lambda_k2, diff_subln_g, w_diff_proj, w_mem_kv, x_qn_g, x_kn_g, w_x_proj, w_out):
    params = (norm_g, mem_norm_g, w_in, conv_dw, conv_dw_b, conv_ln_g, conv_ln_b, w_conv_proj, diff_qn_g,
              diff_kn_g, lambda_q1, lambda_k1, lambda_q2, lambda_k2, diff_subln_g, w_diff_proj, w_mem_kv,
              x_qn_g, x_kn_g, w_x_proj, w_out)
    for l in range(norm_g.shape[0]):
        x = _layer(x, mem, l, *(p[l] for p in params))
    return x
```

```python
import functools
import math

import jax
import jax.numpy as jnp
from jax import lax
import numpy as np
from jax.experimental import pallas as pl
from jax.experimental.pallas import tpu as pltpu

CONV_K = 31
DIFF_HEAD_DIM = 64
DIFF_V_DIM = 2 * DIFF_HEAD_DIM
X_HEADS = 4
N_BRANCH = 3
RMS_EPS = 1e-6
LN_EPS = 1e-5
MASK_VALUE = -1e30
LOG2_E = math.log2(math.e)

V7X_LANES = 128
V7X_SUBLANES = 8
V7X_MXU_DIM = 256
V7X_VMEM_LIMIT_BYTES = 56 * 1024 * 1024

BF16 = jnp.bfloat16
F32 = jnp.float32


def _cparams(sem):
    return pltpu.CompilerParams(dimension_semantics=sem, vmem_limit_bytes=V7X_VMEM_LIMIT_BYTES)


def _sigmoid(x):
    return 1.0 / (1.0 + jnp.exp(-x))


def _silu(x):
    return x * _sigmoid(x)


def _nt_dot(a, b):
    return lax.dot_general(a, b, (((1,), (1,)), ((), ())), preferred_element_type=F32)


PROJ_TN = 1024


def _group_rms(x, gmat, group, gain):
    x2 = (x * x).astype(BF16)
    n = x.shape[1]
    ss = jnp.concatenate([jnp.dot(x2[:, c:c + V7X_MXU_DIM], gmat, preferred_element_type=F32)
                          for c in range(0, n, V7X_MXU_DIM)], axis=1)
    return x * lax.rsqrt(ss * (1.0 / group) + RMS_EPS) * gain


def _norm_proj_kernel(x_ref, g_ref, w_ref, gain_ref, g64_ref, g256_ref, o_ref, *rest, ops):
    n_t = sum(dest is not None for _, _, dest in ops)
    t_refs, wb = rest[:n_t], rest[n_t:]
    if wb:
        w_f32, (w_ref,) = w_ref, wb

        @pl.when(pl.program_id(0) == 0)
        def _():
            w_ref[...] = w_f32[...].astype(w_ref.dtype)

    x = x_ref[...]
    h = (x * lax.rsqrt(jnp.mean(x * x, axis=-1, keepdims=True) + RMS_EPS) * g_ref[...]).astype(BF16)

    def proj(blk):
        return jnp.dot(h, w_ref[:, blk * PROJ_TN:(blk + 1) * PROJ_TN], preferred_element_type=F32)

    ob = 0
    for n, (kind, blks, dest) in enumerate(ops):
        if kind == "glu":
            y = proj(blks[0]) * _sigmoid(proj(blks[1]))
        elif kind == "silu":
            y = _silu(proj(blks[0]))
        elif kind == "sigmoid":
            y = _sigmoid(proj(blks[0]))
        elif kind == "norm64":
            y = _group_rms(proj(blks[0]), g64_ref[...], DIFF_HEAD_DIM, gain_ref[n])
        elif kind == "norm256":
            y = _group_rms(proj(blks[0]), g256_ref[...], V7X_MXU_DIM, gain_ref[n])
        else:
            assert kind == "none", kind
            y = proj(blks[0])
        if dest is None:
            o_ref[:, ob * PROJ_TN:(ob + 1) * PROJ_TN] = y.astype(o_ref.dtype)
            ob += 1
        else:
            t_refs[dest][...] = y.T.astype(t_refs[dest].dtype)


def _norm_proj(x2d, g, w, ops, gains, g64, g256, tm):
    n, d = x2d.shape
    n_t = sum(dest is not None for _, _, dest in ops)
    nout = (len(ops) - n_t) * PROJ_TN

    def resident(a):
        return pl.BlockSpec(a.shape, lambda i: (0,) * a.ndim, pipeline_mode=pl.Buffered(1))

    g2 = g.reshape(1, d)
    cast_in_kernel = w.dtype != BF16
    return pl.pallas_call(
        functools.partial(_norm_proj_kernel, ops=ops),
        grid=(n // tm,),
        in_specs=[pl.BlockSpec((tm, d), lambda i: (i, 0)),
                  resident(g2), resident(w), resident(gains), resident(g64), resident(g256)],
        out_specs=[pl.BlockSpec((tm, nout), lambda i: (i, 0))]
                  + [pl.BlockSpec((PROJ_TN, tm), lambda i: (0, i))] * n_t,
        out_shape=[jax.ShapeDtypeStruct((n, nout), BF16)] + [jax.ShapeDtypeStruct((PROJ_TN, n), BF16)] * n_t,
        scratch_shapes=[pltpu.VMEM(w.shape, BF16)] if cast_in_kernel else [],
        compiler_params=_cparams(("arbitrary" if cast_in_kernel else "parallel",)),
        name="norm_proj",
    )(x2d, g2, w, gains, g64, g256)


def _group_ones(group):
    r = jnp.arange(V7X_MXU_DIM) // group
    return (r[:, None] == r[None, :]).astype(BF16)


CONV_HALO = 32
CONV_ROWS = 32
CONV_COLS = 512


def _conv_kernel(u_ref, gate_ref, dw_ref, dwb_ref, lng_ref, lnb_ref, o_ref, win_ref, y_ref, *, ts):
    j = pl.program_id(1)
    t0 = pl.multiple_of(j * ts, ts)
    d = u_ref.shape[1]

    @pl.when(j == 0)
    def _():
        win_ref[0, 0:CONV_HALO, :] = jnp.zeros((CONV_HALO, d), F32)

    @pl.when(j > 0)
    def _():
        win_ref[0, 0:CONV_HALO, :] = u_ref[pl.ds(t0 - CONV_HALO, CONV_HALO), :].astype(F32)

    win_ref[0, CONV_HALO:CONV_HALO + ts, :] = u_ref[pl.ds(t0, ts), :].astype(F32)

    first = CONV_HALO - (CONV_K - 1)
    span = CONV_HALO + ts - V7X_SUBLANES
    for c in range(d // V7X_LANES):
        cs = slice(c * V7X_LANES, (c + 1) * V7X_LANES)
        x = win_ref[0, :, cs]
        for s in range(1, V7X_SUBLANES):
            x = pltpu.roll(x, x.shape[0] - 1, 0)
            win_ref[s, 0:span, cs] = x[0:span]

    for r0 in range(0, ts, CONV_ROWS):
        for c in range(d // CONV_COLS):
            cs = slice(c * CONV_COLS, (c + 1) * CONV_COLS)
            acc = jnp.zeros((CONV_ROWS, CONV_COLS), F32)
            for s in range(V7X_SUBLANES):
                taps = [k for k in range(CONV_K) if (first + k) % V7X_SUBLANES == s]
                lo = r0 + first + taps[0] - s
                big = win_ref[s, lo:lo + (taps[-1] - taps[0]) + CONV_ROWS, cs]
                for k in taps:
                    w = jnp.concatenate([dw_ref[k, :, cs]] * (CONV_ROWS // V7X_SUBLANES), axis=0)
                    acc = acc + big[k - taps[0]:k - taps[0] + CONV_ROWS] * w
            y_ref[r0:r0 + CONV_ROWS, cs] = acc + dwb_ref[:, cs]

    y = y_ref[...]
    mu = jnp.mean(y, axis=-1, keepdims=True)
    yc = y - mu
    yn = yc * lax.rsqrt(jnp.mean(yc * yc, axis=-1, keepdims=True) + LN_EPS)
    yn = yn * lng_ref[...] + lnb_ref[...]
    o_ref[...] = (_silu(yn) * gate_ref[...].astype(F32)).astype(o_ref.dtype)


def _conv_branch(acts, u_block, gate_block, dw, dwb, lng, lnb, batch, seq, ts):
    n, d = acts.shape[0], dw.shape[1]
    nt = seq // ts
    return pl.pallas_call(
        functools.partial(_conv_kernel, ts=ts),
        grid=(batch, nt),
        in_specs=[pl.BlockSpec((seq, d), lambda b, j: (b, u_block)),
                  pl.BlockSpec((ts, d), lambda b, j: (b * nt + j, gate_block)),
                  pl.BlockSpec((CONV_K, V7X_SUBLANES, d), lambda b, j: (0, 0, 0)),
                  pl.BlockSpec((1, d), lambda b, j: (0, 0)),
                  pl.BlockSpec((1, d), lambda b, j: (0, 0)),
                  pl.BlockSpec((1, d), lambda b, j: (0, 0))],
        out_specs=pl.BlockSpec((ts, d), lambda b, j: (b * nt + j, 0)),
        out_shape=jax.ShapeDtypeStruct((n, d), BF16),
        scratch_shapes=[pltpu.VMEM((V7X_SUBLANES, CONV_HALO + ts, d), F32), pltpu.VMEM((ts, d), F32)],
        compiler_params=_cparams(("parallel", "arbitrary")),
        name="conv_branch",
    )(acts, acts, jnp.broadcast_to(dw[:, None, :], (CONV_K, V7X_SUBLANES, d)),
      dwb.reshape(1, d), lng.reshape(1, d), lnb.reshape(1, d))


EXTRA_ROWS = 16
PIECES = 3
POS_RADIX = 128
BF16_NORM_MARGIN = 1.02
MAX_FAST_BOUND = 40.0


def _pieces(v):
    out = []
    for _ in range(PIECES):
        piece = v.astype(BF16).astype(F32)
        out.append(piece)
        v = v - piece
    return out


def _position_tables(heads, seq):
    d = DIFF_HEAD_DIM
    rest = (2.0 ** (-8.0 * np.arange(1, heads + 1, dtype=np.float32) / heads) * np.float32(LOG2_E)).astype(np.float32)
    slopes = []
    for _ in range(PIECES):
        piece = rest.astype(BF16).astype(np.float32)
        slopes.append(piece)
        rest = rest - piece
    pos = np.arange(seq)
    hi = np.broadcast_to((pos // POS_RADIX).astype(np.float32), (heads, seq))
    lo = np.broadcast_to((pos % POS_RADIX).astype(np.float32), (heads, seq))
    const = lambda v: np.broadcast_to(v[:, None], (heads, seq))
    zero = np.zeros((heads, seq), np.float32)
    q_rows = ([const(POS_RADIX * p) for p in slopes] + [const(p) for p in slopes] + [hi] * PIECES + [lo] * PIECES
              + [zero] * (PIECES + 1))
    k_rows = ([hi] * PIECES + [lo] * PIECES + [const(-POS_RADIX * p) for p in slopes] + [const(-p) for p in slopes]
              + [zero + 1.0] * PIECES + [zero])
    q_extra = np.stack(q_rows, axis=1)
    k_extra = np.stack(k_rows, axis=2)
    k_tab = np.zeros((heads, 2, seq, DIFF_V_DIM), np.float32)
    k_tab[:, 0, :, d:d + EXTRA_ROWS] = k_extra
    k_tab[:, 1, :, :EXTRA_ROWS] = k_extra
    return jnp.asarray(q_extra.astype(BF16)), jnp.asarray(k_tab.astype(BF16))


def _score_bound(qn_g, kn_g):
    bound = (DIFF_HEAD_DIM ** 0.5 * LOG2_E * BF16_NORM_MARGIN) * jnp.max(jnp.abs(qn_g)) * jnp.max(jnp.abs(kn_g))
    fast = bound < MAX_FAST_BOUND
    neg_b = _pieces(jnp.where(fast, -bound, 0.0))
    rows = [jnp.zeros((), F32)] * (4 * PIECES) + neg_b + [jnp.zeros((), F32)]
    tile = jnp.broadcast_to(jnp.stack(rows)[:, None], (EXTRA_ROWS, V7X_LANES))
    return tile, fast.astype(jnp.int32).reshape(1)


HEADS_PER_STEP = 2


def _diff_attn_kernel(fast_ref, bound_ref, qx_ref, kx_ref, qt_in_ref, k_ref, vt_ref, gate_ref, lam_ref, subg_ref,
                      o_ref, km_ref, qt_ref, mask_ref, *, tq, lambda_init):
    seq = k_ref.shape[0]
    nq = seq // tq
    d = DIFF_HEAD_DIM
    first_half = lax.broadcasted_iota(jnp.int32, (tq, DIFF_V_DIM), 1) < d
    pad = jnp.zeros((DIFF_V_DIM - d - EXTRA_ROWS, tq), qt_ref.dtype)
    bound_rows = jnp.concatenate([bound_ref[...]] * (tq // V7X_LANES), axis=1)

    for hh in range(HEADS_PER_STEP):
        hl = slice(hh * DIFF_V_DIM, (hh + 1) * DIFF_V_DIM)
        for c in range(nq):
            rows = slice(c * tq, (c + 1) * tq)
            k = k_ref[rows, hl]
            km_ref[hh, 0, rows, :] = jnp.where(first_half, k, kx_ref[hh, 0, rows, :])
            km_ref[hh, 1, rows, :] = jnp.where(first_half, kx_ref[hh, 1, rows, :], k)
        for i in range(nq):
            cols = slice(i * tq, (i + 1) * tq)
            extra = (qx_ref[hh, :, cols].astype(F32) + bound_rows).astype(qt_ref.dtype)
            qt_ref[hh, i, 0] = jnp.concatenate([qt_in_ref[hh * DIFF_V_DIM:hh * DIFF_V_DIM + d, cols], extra, pad], axis=0)
            qt_ref[hh, i, 1] = jnp.concatenate([extra, pad, qt_in_ref[hh * DIFF_V_DIM + d:(hh + 1) * DIFF_V_DIM, cols]],
                                               axis=0)

    kk = lax.broadcasted_iota(jnp.int32, (tq, tq), 0)
    qq = lax.broadcasted_iota(jnp.int32, (tq, tq), 1)
    mask_ref[...] = jnp.where(kk <= qq, 0.0, MASK_VALUE)

    lam_v = lam_ref[...]
    lam = (jnp.exp(jnp.sum(lam_v[0:1] * lam_v[1:2], axis=-1, keepdims=True))
           - jnp.exp(jnp.sum(lam_v[2:3] * lam_v[3:4], axis=-1, keepdims=True)) + lambda_init)

    def scores(hh, i):
        keys = (i + 1) * tq
        return [jnp.dot(km_ref[hh, mp, :keys, :], qt_ref[hh, i, mp], preferred_element_type=F32)
                for mp in range(2)]

    def attend(bounded):
        work = [(hh, i) for hh in range(HEADS_PER_STEP) for i in range(nq)]
        x_next = scores(*work[0])
        for n, (hh, i) in enumerate(work):
            x_cur = x_next
            if n + 1 < len(work):
                x_next = scores(*work[n + 1])
            keys = (i + 1) * tq
            vt = vt_ref[hh * DIFF_V_DIM:(hh + 1) * DIFF_V_DIM, :keys]
            heads_out = []
            for x in x_cur:
                x = (jnp.concatenate([x[:keys - tq], x[keys - tq:] + mask_ref[...]], axis=0) if i
                     else x + mask_ref[...])
                p = jnp.exp2(x) if bounded else jnp.exp2(x - jnp.max(x, axis=0, keepdims=True))
                pv = jnp.dot(vt, p.astype(vt.dtype), preferred_element_type=F32)
                heads_out.append(pv / jnp.sum(p, axis=0, keepdims=True))
            rows = slice(i * tq, (i + 1) * tq)
            hl = slice(hh * DIFF_V_DIM, (hh + 1) * DIFF_V_DIM)
            o = heads_out[0] - lam * heads_out[1]
            o = o * lax.rsqrt(jnp.mean(o * o, axis=0, keepdims=True) + RMS_EPS)
            o = o.T * (subg_ref[...] * (1.0 - lambda_init))
            o_ref[rows, hl] = (o * gate_ref[rows, hl].astype(F32)).astype(o_ref.dtype)

    fast = fast_ref[0] == 1
    pl.when(fast)(functools.partial(attend, True))
    pl.when(jnp.logical_not(fast))(functools.partial(attend, False))


def _diff_attention(acts, q_t, v_t, k_block, gate_block, tables, lam_vecs, subg, batch, seq, heads, tq, lambda_init):
    n = acts.shape[0]
    nq = seq // tq
    hps = HEADS_PER_STEP
    q_extra, k_tab, bound_tile, fast = tables
    kb, gb = (b * heads // hps for b in (k_block, gate_block))
    kernel = functools.partial(_diff_attn_kernel, tq=tq, lambda_init=lambda_init)
    head_spec = lambda first: pl.BlockSpec((seq, hps * DIFF_V_DIM), lambda b, h: (b, first + h))
    t_spec = pl.BlockSpec((hps * DIFF_V_DIM, seq), lambda b, h: (h, b))
    return pl.pallas_call(
        kernel,
        grid=(batch, heads // hps),
        in_specs=[pl.BlockSpec(memory_space=pltpu.SMEM),
                  pl.BlockSpec((EXTRA_ROWS, V7X_LANES), lambda b, h: (0, 0)),
                  pl.BlockSpec((hps, EXTRA_ROWS, seq), lambda b, h: (h, 0, 0)),
                  pl.BlockSpec((hps, 2, seq, DIFF_V_DIM), lambda b, h: (h, 0, 0, 0)),
                  t_spec, head_spec(kb), t_spec, head_spec(gb),
                  pl.BlockSpec((4, DIFF_HEAD_DIM), lambda b, h: (0, 0)),
                  pl.BlockSpec((1, DIFF_V_DIM), lambda b, h: (0, 0))],
        out_specs=head_spec(0),
        out_shape=jax.ShapeDtypeStruct((n, heads * DIFF_V_DIM), BF16),
        scratch_shapes=[pltpu.VMEM((hps, 2, seq, DIFF_V_DIM), BF16),
                        pltpu.VMEM((hps, nq, 2, DIFF_V_DIM, tq), BF16),
                        pltpu.VMEM((tq, tq), F32)],
        compiler_params=_cparams(("parallel", "parallel")),
        name="diff_attention",
    )(fast, bound_tile, q_extra, k_tab, q_t, acts, v_t, acts, lam_vecs, subg.reshape(1, DIFF_V_DIM))


def _cross_attend(q_ref, k_ref, v_ref, gate_ref, heads):
    hd = q_ref.shape[1] // heads
    outs = []
    for h in range(heads):
        sl = slice(h * hd, (h + 1) * hd)
        s = _nt_dot(q_ref[:, sl], k_ref[:, sl])
        m = jnp.max(s, axis=-1, keepdims=True)
        p = jnp.exp(s - m)
        l = jnp.sum(p, axis=-1, keepdims=True)
        o = jnp.dot(p.astype(BF16), v_ref[:, sl], preferred_element_type=F32) / l
        outs.append(o * gate_ref[:, sl].astype(F32))
    return jnp.concatenate(outs, axis=1)


def _merge_kernel(x_ref, ca_ref, da_ref, xq_ref, xk_ref, xv_ref, xg_ref, g0_ref, g1_ref, g2_ref,
                  wc_ref, wd_ref, wx_ref, wo_ref, o_ref, wb_ref, *, x_heads):
    @pl.when(pl.program_id(0) == 0)
    def _():
        for c, w_ref in enumerate((wc_ref, wd_ref, wx_ref, wo_ref)):
            wb_ref[c] = w_ref[...].astype(wb_ref.dtype)

    xa = _cross_attend(xq_ref, xk_ref, xv_ref, xg_ref, x_heads).astype(BF16)
    y = g0_ref[...].astype(F32) * jnp.dot(ca_ref[...], wb_ref[0], preferred_element_type=F32)
    y = y + g1_ref[...].astype(F32) * jnp.dot(da_ref[...], wb_ref[1], preferred_element_type=F32)
    y = y + g2_ref[...].astype(F32) * jnp.dot(xa, wb_ref[2], preferred_element_type=F32)
    o_ref[...] = x_ref[...] + jnp.dot(y.astype(BF16), wb_ref[3], preferred_element_type=F32)


def _merge_out(x2d, ca, da, acts, xq_block, xgate_block, gate_block0, mem_kv, wc, wd, wx, wo, seq, mem_len, tm):
    n, d = x2d.shape
    tiles_per_seq = seq // tm
    row = lambda i: (i, 0)
    fixed = lambda i: (0, 0)
    act_spec = pl.BlockSpec((tm, d), row)
    acts_spec = lambda blk: pl.BlockSpec((tm, d), lambda i: (i, blk))
    mem_spec = lambda half: pl.BlockSpec((mem_len, d), lambda i: (i // tiles_per_seq, half))
    w_spec = pl.BlockSpec((d, d), fixed, pipeline_mode=pl.Buffered(1))
    return pl.pallas_call(
        functools.partial(_merge_kernel, x_heads=X_HEADS),
        grid=(n // tm,),
        in_specs=[pl.BlockSpec((tm, d), row), act_spec, act_spec,
                  acts_spec(xq_block), mem_spec(0), mem_spec(1), acts_spec(xgate_block),
                  *[acts_spec(gate_block0 + c) for c in range(N_BRANCH)],
                  w_spec, w_spec, w_spec, w_spec],
        out_specs=pl.BlockSpec((tm, d), row),
        out_shape=jax.ShapeDtypeStruct((n, d), x2d.dtype),
        scratch_shapes=[pltpu.VMEM((4, d, d), BF16)],
        compiler_params=_cparams(("arbitrary",)),
        name="merge_out",
    )(x2d, ca, da, acts, mem_kv, mem_kv, acts, acts, acts, acts, wc, wd, wx, wo)


def _layer(x, mem, l, norm_g, mem_norm_g, w_in, conv_dw, conv_dw_b, conv_ln_g, conv_ln_b, w_conv_proj,
           diff_qn_g, diff_kn_g, lambda_q1, lambda_k1, lambda_q2, lambda_k2, diff_subln_g, w_diff_proj,
           w_mem_kv, x_qn_g, x_kn_g, w_x_proj, w_out):
    batch, seq, d = x.shape
    mem_len = mem.shape[1]
    heads = d // DIFF_V_DIM
    x_head_dim = d // X_HEADS
    assert x_head_dim == V7X_MXU_DIM and d % V7X_MXU_DIM == 0
    n = batch * seq
    x2d = x.reshape(n, d)
    lambda_init = 0.8 - 0.6 * math.exp(-0.3 * l)
    attn_tables = _position_tables(heads, seq) + _score_bound(diff_qn_g, diff_kn_g)

    assert d == PROJ_TN
    g64 = _group_ones(DIFF_HEAD_DIM)
    g256 = _group_ones(V7X_MXU_DIM)
    ones = jnp.ones((d,), F32)

    ops = (("glu", (0, 1), None), ("silu", (2,), None), ("norm64", (3,), 0), ("norm64", (4,), None),
           ("none", (5,), 1), ("silu", (6,), None), ("norm256", (7,), None), ("silu", (8,), None),
           ("sigmoid", (9,), None), ("sigmoid", (10,), None), ("sigmoid", (11,), None))
    U, C_GATE, D_K, D_GATE, X_Q, X_GATE, MERGE = range(7)
    OP_D_Q, OP_D_K, OP_X_Q = 2, 3, 6
    gain_rows = {OP_D_Q: jnp.tile(diff_qn_g, d // DIFF_HEAD_DIM) * (DIFF_HEAD_DIM ** -0.5 * LOG2_E),
                 OP_D_K: jnp.tile(diff_kn_g, d // DIFF_HEAD_DIM),
                 OP_X_Q: jnp.tile(x_qn_g, X_HEADS) * (x_head_dim ** -0.5)}
    gains = jnp.stack([gain_rows.get(b, ones) for b in range(len(ops))]).reshape(len(ops), 1, d)
    acts, q_t, v_t = _norm_proj(x2d, norm_g, w_in.astype(BF16), ops, gains, g64, g256, tm=256)

    mem_ops = (("norm256", (0,), None), ("none", (1,), None))
    mem_gains = jnp.stack([jnp.tile(x_kn_g, X_HEADS), ones]).reshape(2, 1, d)
    mem_kv, = _norm_proj(mem.reshape(batch * mem_len, d), mem_norm_g, w_mem_kv, mem_ops, mem_gains,
                         g64, g256, tm=256)

    conv_act = _conv_branch(acts, U, C_GATE, conv_dw, conv_dw_b, conv_ln_g, conv_ln_b, batch, seq, ts=512)
    lam_vecs = jnp.stack([lambda_q1, lambda_k1, lambda_q2, lambda_k2])
    diff_act = _diff_attention(acts, q_t, v_t, D_K, D_GATE, attn_tables, lam_vecs, diff_subln_g,
                               batch, seq, heads, tq=256, lambda_init=lambda_init)
    out = _merge_out(x2d, conv_act, diff_act, acts, X_Q, X_GATE, MERGE, mem_kv,
                     w_conv_proj, w_diff_proj, w_x_proj, w_out, seq, mem_len, tm=512)
    return out.reshape(batch, seq, d)


def kernel(x, mem, norm_g, mem_norm_g, w_in, conv_dw, conv_dw_b, conv_ln_g, conv_ln_b, w_conv_proj, diff_qn_g, diff_kn_g, lambda_q1, lambda_k1, lambda_q2, lambda_k2, diff_subln_g, w_diff_proj, w_mem_kv, x_qn_g, x_kn_g, w_x_proj, w_out):
    params = (norm_g, mem_norm_g, w_in, conv_dw, conv_dw_b, conv_ln_g, conv_ln_b, w_conv_proj, diff_qn_g,
              diff_kn_g, lambda_q1, lambda_k1, lambda_q2, lambda_k2, diff_subln_g, w_diff_proj, w_mem_kv,
              x_qn_g, x_kn_g, w_x_proj, w_out)
    for l in range(norm_g.shape[0]):
        x = _layer(x, mem, l, *(p[l] for p in params))
    return x
```

```python
import functools
import math

import jax
import jax.numpy as jnp
from jax import lax
import numpy as np
from jax.experimental import pallas as pl
from jax.experimental.pallas import tpu as pltpu

CONV_K = 31
DIFF_HEAD_DIM = 64
DIFF_V_DIM = 2 * DIFF_HEAD_DIM
X_HEADS = 4
N_BRANCH = 3
RMS_EPS = 1e-6
LN_EPS = 1e-5
MASK_VALUE = -1e30
LOG2_E = math.log2(math.e)

V7X_LANES = 128
V7X_SUBLANES = 8
V7X_MXU_DIM = 256
V7X_VMEM_LIMIT_BYTES = 56 * 1024 * 1024

PROJ_ROWS = 256
CONV_ROWS_PER_STEP = 256
ATTN_Q_ROWS = 256
MERGE_ROWS = 512

BF16 = jnp.bfloat16
F32 = jnp.float32


def _cparams(sem):
    return pltpu.CompilerParams(dimension_semantics=sem, vmem_limit_bytes=V7X_VMEM_LIMIT_BYTES)


def _sigmoid(x):
    return 1.0 / (1.0 + jnp.exp(-x))


def _silu(x):
    return x * _sigmoid(x)


def _nt_dot(a, b):
    return lax.dot_general(a, b, (((1,), (1,)), ((), ())), preferred_element_type=F32)


PROJ_TN = 1024


def _group_rms(x, gmat, group, gain):
    x2 = (x * x).astype(BF16)
    n = x.shape[1]
    ss = jnp.concatenate([jnp.dot(x2[:, c:c + V7X_MXU_DIM], gmat, preferred_element_type=F32)
                          for c in range(0, n, V7X_MXU_DIM)], axis=1)
    return x * lax.rsqrt(ss * (1.0 / group) + RMS_EPS) * gain


def _norm_proj_kernel(x_ref, g_ref, w_ref, gain_ref, g64_ref, g256_ref, o_ref, *rest, ops):
    n_t = sum(dest is not None for _, _, dest in ops)
    t_refs, wb = rest[:n_t], rest[n_t:]
    if wb:
        w_f32, (w_ref,) = w_ref, wb

        @pl.when(pl.program_id(0) == 0)
        def _():
            w_ref[...] = w_f32[...].astype(w_ref.dtype)

    x = x_ref[...]
    h = (x * lax.rsqrt(jnp.mean(x * x, axis=-1, keepdims=True) + RMS_EPS) * g_ref[...]).astype(BF16)

    def proj(blk):
        return jnp.dot(h, w_ref[:, blk * PROJ_TN:(blk + 1) * PROJ_TN], preferred_element_type=F32)

    ob = 0
    for n, (kind, blks, dest) in enumerate(ops):
        if kind == "glu":
            y = proj(blks[0]) * _sigmoid(proj(blks[1]))
        elif kind == "silu":
            y = _silu(proj(blks[0]))
        elif kind == "sigmoid":
            y = _sigmoid(proj(blks[0]))
        elif kind == "norm64":
            y = _group_rms(proj(blks[0]), g64_ref[...], DIFF_HEAD_DIM, gain_ref[n])
        elif kind == "norm256":
            y = _group_rms(proj(blks[0]), g256_ref[...], V7X_MXU_DIM, gain_ref[n])
        else:
            assert kind == "none", kind
            y = proj(blks[0])
        if dest is None:
            o_ref[:, ob * PROJ_TN:(ob + 1) * PROJ_TN] = y.astype(o_ref.dtype)
            ob += 1
        else:
            t_refs[dest][...] = y.T.astype(t_refs[dest].dtype)


def _norm_proj(x2d, g, w, ops, gains, g64, g256, tm):
    n, d = x2d.shape
    n_t = sum(dest is not None for _, _, dest in ops)
    nout = (len(ops) - n_t) * PROJ_TN

    def resident(a):
        return pl.BlockSpec(a.shape, lambda i: (0,) * a.ndim, pipeline_mode=pl.Buffered(1))

    g2 = g.reshape(1, d)
    cast_in_kernel = w.dtype != BF16
    return pl.pallas_call(
        functools.partial(_norm_proj_kernel, ops=ops),
        grid=(n // tm,),
        in_specs=[pl.BlockSpec((tm, d), lambda i: (i, 0)),
                  resident(g2), resident(w), resident(gains), resident(g64), resident(g256)],
        out_specs=[pl.BlockSpec((tm, nout), lambda i: (i, 0))]
                  + [pl.BlockSpec((PROJ_TN, tm), lambda i: (0, i))] * n_t,
        out_shape=[jax.ShapeDtypeStruct((n, nout), BF16)] + [jax.ShapeDtypeStruct((PROJ_TN, n), BF16)] * n_t,
        scratch_shapes=[pltpu.VMEM(w.shape, BF16)] if cast_in_kernel else [],
        compiler_params=_cparams(("arbitrary" if cast_in_kernel else "parallel",)),
        name="norm_proj",
    )(x2d, g2, w, gains, g64, g256)


def _group_ones(group):
    r = jnp.arange(V7X_MXU_DIM) // group
    return (r[:, None] == r[None, :]).astype(BF16)


CONV_HALO = 32
CONV_ROWS = 32
CONV_COLS = 512


def _conv_kernel(u_ref, gate_ref, dw_ref, dwb_ref, lng_ref, lnb_ref, o_ref, win_ref, y_ref, *, ts):
    j = pl.program_id(1)
    t0 = pl.multiple_of(j * ts, ts)
    d = u_ref.shape[1]

    @pl.when(j == 0)
    def _():
        win_ref[0, 0:CONV_HALO, :] = jnp.zeros((CONV_HALO, d), F32)

    @pl.when(j > 0)
    def _():
        win_ref[0, 0:CONV_HALO, :] = u_ref[pl.ds(t0 - CONV_HALO, CONV_HALO), :].astype(F32)

    win_ref[0, CONV_HALO:CONV_HALO + ts, :] = u_ref[pl.ds(t0, ts), :].astype(F32)

    first = CONV_HALO - (CONV_K - 1)
    span = CONV_HALO + ts - V7X_SUBLANES
    for c in range(d // V7X_LANES):
        cs = slice(c * V7X_LANES, (c + 1) * V7X_LANES)
        x = win_ref[0, :, cs]
        for s in range(1, V7X_SUBLANES):
            x = pltpu.roll(x, x.shape[0] - 1, 0)
            win_ref[s, 0:span, cs] = x[0:span]

    for r0 in range(0, ts, CONV_ROWS):
        for c in range(d // CONV_COLS):
            cs = slice(c * CONV_COLS, (c + 1) * CONV_COLS)
            acc = jnp.zeros((CONV_ROWS, CONV_COLS), F32)
            for s in range(V7X_SUBLANES):
                taps = [k for k in range(CONV_K) if (first + k) % V7X_SUBLANES == s]
                lo = r0 + first + taps[0] - s
                big = win_ref[s, lo:lo + (taps[-1] - taps[0]) + CONV_ROWS, cs]
                for k in taps:
                    w = jnp.concatenate([dw_ref[k, :, cs]] * (CONV_ROWS // V7X_SUBLANES), axis=0)
                    acc = acc + big[k - taps[0]:k - taps[0] + CONV_ROWS] * w
            y_ref[r0:r0 + CONV_ROWS, cs] = acc + dwb_ref[:, cs]

    y = y_ref[...]
    mu = jnp.mean(y, axis=-1, keepdims=True)
    yc = y - mu
    yn = yc * lax.rsqrt(jnp.mean(yc * yc, axis=-1, keepdims=True) + LN_EPS)
    yn = yn * lng_ref[...] + lnb_ref[...]
    o_ref[...] = (_silu(yn) * gate_ref[...].astype(F32)).astype(o_ref.dtype)


def _conv_branch(acts, u_block, gate_block, dw, dwb, lng, lnb, batch, seq, ts):
    n, d = acts.shape[0], dw.shape[1]
    nt = seq // ts
    return pl.pallas_call(
        functools.partial(_conv_kernel, ts=ts),
        grid=(batch, nt),
        in_specs=[pl.BlockSpec((seq, d), lambda b, j: (b, u_block)),
                  pl.BlockSpec((ts, d), lambda b, j: (b * nt + j, gate_block)),
                  pl.BlockSpec((CONV_K, V7X_SUBLANES, d), lambda b, j: (0, 0, 0)),
                  pl.BlockSpec((1, d), lambda b, j: (0, 0)),
                  pl.BlockSpec((1, d), lambda b, j: (0, 0)),
                  pl.BlockSpec((1, d), lambda b, j: (0, 0))],
        out_specs=pl.BlockSpec((ts, d), lambda b, j: (b * nt + j, 0)),
        out_shape=jax.ShapeDtypeStruct((n, d), BF16),
        scratch_shapes=[pltpu.VMEM((V7X_SUBLANES, CONV_HALO + ts, d), F32), pltpu.VMEM((ts, d), F32)],
        compiler_params=_cparams(("parallel", "arbitrary")),
        name="conv_branch",
    )(acts, acts, jnp.broadcast_to(dw[:, None, :], (CONV_K, V7X_SUBLANES, d)),
      dwb.reshape(1, d), lng.reshape(1, d), lnb.reshape(1, d))


EXTRA_ROWS = 16
PIECES = 3
POS_RADIX = 128
BF16_NORM_MARGIN = 1.02
MAX_FAST_BOUND = 40.0


def _pieces(v):
    out = []
    for _ in range(PIECES):
        piece = v.astype(BF16).astype(F32)
        out.append(piece)
        v = v - piece
    return out


def _position_tables(heads, seq):
    d = DIFF_HEAD_DIM
    rest = (2.0 ** (-8.0 * np.arange(1, heads + 1, dtype=np.float32) / heads) * np.float32(LOG2_E)).astype(np.float32)
    slopes = []
    for _ in range(PIECES):
        piece = rest.astype(BF16).astype(np.float32)
        slopes.append(piece)
        rest = rest - piece
    pos = np.arange(seq)
    hi = np.broadcast_to((pos // POS_RADIX).astype(np.float32), (heads, seq))
    lo = np.broadcast_to((pos % POS_RADIX).astype(np.float32), (heads, seq))
    const = lambda v: np.broadcast_to(v[:, None], (heads, seq))
    zero = np.zeros((heads, seq), np.float32)
    q_rows = ([const(POS_RADIX * p) for p in slopes] + [const(p) for p in slopes] + [hi] * PIECES + [lo] * PIECES
              + [zero] * (PIECES + 1))
    k_rows = ([hi] * PIECES + [lo] * PIECES + [const(-POS_RADIX * p) for p in slopes] + [const(-p) for p in slopes]
              + [zero + 1.0] * PIECES + [zero])
    q_extra = np.stack(q_rows, axis=1)
    k_extra = np.stack(k_rows, axis=2)
    k_tab = np.zeros((heads, 2, seq, DIFF_V_DIM), np.float32)
    k_tab[:, 0, :, d:d + EXTRA_ROWS] = k_extra
    k_tab[:, 1, :, :EXTRA_ROWS] = k_extra
    return jnp.asarray(q_extra.astype(BF16)), jnp.asarray(k_tab.astype(BF16))


def _score_bound(qn_g, kn_g):
    bound = (DIFF_HEAD_DIM ** 0.5 * LOG2_E * BF16_NORM_MARGIN) * jnp.max(jnp.abs(qn_g)) * jnp.max(jnp.abs(kn_g))
    fast = bound < MAX_FAST_BOUND
    neg_b = _pieces(jnp.where(fast, -bound, 0.0))
    rows = [jnp.zeros((), F32)] * (4 * PIECES) + neg_b + [jnp.zeros((), F32)]
    tile = jnp.broadcast_to(jnp.stack(rows)[:, None], (EXTRA_ROWS, V7X_LANES))
    return tile, fast.astype(jnp.int32).reshape(1)


HEADS_PER_STEP = 2


def _diff_attn_kernel(fast_ref, bound_ref, qx_ref, kx_ref, qt_in_ref, k_ref, vt_ref, gate_ref, lam_ref, subg_ref,
                      o_ref, km_ref, qt_ref, mask_ref, *, tq, lambda_init):
    seq = k_ref.shape[0]
    nq = seq // tq
    d = DIFF_HEAD_DIM
    first_half = lax.broadcasted_iota(jnp.int32, (tq, DIFF_V_DIM), 1) < d
    pad = jnp.zeros((DIFF_V_DIM - d - EXTRA_ROWS, tq), qt_ref.dtype)
    bound_rows = jnp.concatenate([bound_ref[...]] * (tq // V7X_LANES), axis=1)

    for hh in range(HEADS_PER_STEP):
        hl = slice(hh * DIFF_V_DIM, (hh + 1) * DIFF_V_DIM)
        for c in range(nq):
            rows = slice(c * tq, (c + 1) * tq)
            k = k_ref[rows, hl]
            km_ref[hh, 0, rows, :] = jnp.where(first_half, k, kx_ref[hh, 0, rows, :])
            km_ref[hh, 1, rows, :] = jnp.where(first_half, kx_ref[hh, 1, rows, :], k)
        for i in range(nq):
            cols = slice(i * tq, (i + 1) * tq)
            extra = (qx_ref[hh, :, cols].astype(F32) + bound_rows).astype(qt_ref.dtype)
            qt_ref[hh, i, 0] = jnp.concatenate([qt_in_ref[hh * DIFF_V_DIM:hh * DIFF_V_DIM + d, cols], extra, pad], axis=0)
            qt_ref[hh, i, 1] = jnp.concatenate([extra, pad, qt_in_ref[hh * DIFF_V_DIM + d:(hh + 1) * DIFF_V_DIM, cols]],
                                               axis=0)

    kk = lax.broadcasted_iota(jnp.int32, (tq, tq), 0)
    qq = lax.broadcasted_iota(jnp.int32, (tq, tq), 1)
    mask_ref[...] = jnp.where(kk <= qq, 0.0, MASK_VALUE)

    lam_v = lam_ref[...]
    lam = (jnp.exp(jnp.sum(lam_v[0:1] * lam_v[1:2], axis=-1, keepdims=True))
           - jnp.exp(jnp.sum(lam_v[2:3] * lam_v[3:4], axis=-1, keepdims=True)) + lambda_init)

    def scores(hh, i):
        keys = (i + 1) * tq
        return [jnp.dot(km_ref[hh, mp, :keys, :], qt_ref[hh, i, mp], preferred_element_type=F32)
                for mp in range(2)]

    def attend(bounded):
        work = [(hh, i) for hh in range(HEADS_PER_STEP) for i in range(nq)]
        x_next = scores(*work[0])
        for n, (hh, i) in enumerate(work):
            x_cur = x_next
            if n + 1 < len(work):
                x_next = scores(*work[n + 1])
            keys = (i + 1) * tq
            vt = vt_ref[hh * DIFF_V_DIM:(hh + 1) * DIFF_V_DIM, :keys]
            heads_out = []
            for x in x_cur:
                x = (jnp.concatenate([x[:keys - tq], x[keys - tq:] + mask_ref[...]], axis=0) if i
                     else x + mask_ref[...])
                p = jnp.exp2(x) if bounded else jnp.exp2(x - jnp.max(x, axis=0, keepdims=True))
                pv = jnp.dot(vt, p.astype(vt.dtype), preferred_element_type=F32)
                heads_out.append(pv / jnp.sum(p, axis=0, keepdims=True))
            rows = slice(i * tq, (i + 1) * tq)
            hl = slice(hh * DIFF_V_DIM, (hh + 1) * DIFF_V_DIM)
            o = heads_out[0] - lam * heads_out[1]
            o = o * lax.rsqrt(jnp.mean(o * o, axis=0, keepdims=True) + RMS_EPS)
            o = o.T * (subg_ref[...] * (1.0 - lambda_init))
            o_ref[rows, hl] = (o * gate_ref[rows, hl].astype(F32)).astype(o_ref.dtype)

    fast = fast_ref[0] == 1
    pl.when(fast)(functools.partial(attend, True))
    pl.when(jnp.logical_not(fast))(functools.partial(attend, False))


def _diff_attention(acts, q_t, v_t, k_block, gate_block, tables, lam_vecs, subg, batch, seq, heads, tq, lambda_init):
    n = acts.shape[0]
    nq = seq // tq
    hps = HEADS_PER_STEP
    q_extra, k_tab, bound_tile, fast = tables
    kb, gb = (b * heads // hps for b in (k_block, gate_block))
    kernel = functools.partial(_diff_attn_kernel, tq=tq, lambda_init=lambda_init)
    head_spec = lambda first: pl.BlockSpec((seq, hps * DIFF_V_DIM), lambda b, h: (b, first + h))
    t_spec = pl.BlockSpec((hps * DIFF_V_DIM, seq), lambda b, h: (h, b))
    return pl.pallas_call(
        kernel,
        grid=(batch, heads // hps),
        in_specs=[pl.BlockSpec(memory_space=pltpu.SMEM),
                  pl.BlockSpec((EXTRA_ROWS, V7X_LANES), lambda b, h: (0, 0)),
                  pl.BlockSpec((hps, EXTRA_ROWS, seq), lambda b, h: (h, 0, 0)),
                  pl.BlockSpec((hps, 2, seq, DIFF_V_DIM), lambda b, h: (h, 0, 0, 0)),
                  t_spec, head_spec(kb), t_spec, head_spec(gb),
                  pl.BlockSpec((4, DIFF_HEAD_DIM), lambda b, h: (0, 0)),
                  pl.BlockSpec((1, DIFF_V_DIM), lambda b, h: (0, 0))],
        out_specs=head_spec(0),
        out_shape=jax.ShapeDtypeStruct((n, heads * DIFF_V_DIM), BF16),
        scratch_shapes=[pltpu.VMEM((hps, 2, seq, DIFF_V_DIM), BF16),
                        pltpu.VMEM((hps, nq, 2, DIFF_V_DIM, tq), BF16),
                        pltpu.VMEM((tq, tq), F32)],
        compiler_params=_cparams(("parallel", "parallel")),
        name="diff_attention",
    )(fast, bound_tile, q_extra, k_tab, q_t, acts, v_t, acts, lam_vecs, subg.reshape(1, DIFF_V_DIM))


def _cross_attend(q_ref, k_ref, v_ref, gate_ref, heads):
    hd = q_ref.shape[1] // heads
    outs = []
    for h in range(heads):
        sl = slice(h * hd, (h + 1) * hd)
        s = _nt_dot(q_ref[:, sl], k_ref[:, sl])
        m = jnp.max(s, axis=-1, keepdims=True)
        p = jnp.exp(s - m)
        l = jnp.sum(p, axis=-1, keepdims=True)
        o = jnp.dot(p.astype(BF16), v_ref[:, sl], preferred_element_type=F32) / l
        outs.append(o * gate_ref[:, sl].astype(F32))
    return jnp.concatenate(outs, axis=1)


def _merge_kernel(x_ref, ca_ref, da_ref, xq_ref, xk_ref, xv_ref, xg_ref, g0_ref, g1_ref, g2_ref,
                  wc_ref, wd_ref, wx_ref, wo_ref, o_ref, wb_ref, *, x_heads):
    @pl.when(pl.program_id(0) == 0)
    def _():
        for c, w_ref in enumerate((wc_ref, wd_ref, wx_ref, wo_ref)):
            wb_ref[c] = w_ref[...].astype(wb_ref.dtype)

    xa = _cross_attend(xq_ref, xk_ref, xv_ref, xg_ref, x_heads).astype(BF16)
    y = g0_ref[...].astype(F32) * jnp.dot(ca_ref[...], wb_ref[0], preferred_element_type=F32)
    y = y + g1_ref[...].astype(F32) * jnp.dot(da_ref[...], wb_ref[1], preferred_element_type=F32)
    y = y + g2_ref[...].astype(F32) * jnp.dot(xa, wb_ref[2], preferred_element_type=F32)
    o_ref[...] = x_ref[...] + jnp.dot(y.astype(BF16), wb_ref[3], preferred_element_type=F32)


def _merge_out(x2d, ca, da, acts, xq_block, xgate_block, gate_block0, mem_kv, wc, wd, wx, wo, seq, mem_len, tm):
    n, d = x2d.shape
    tiles_per_seq = seq // tm
    row = lambda i: (i, 0)
    fixed = lambda i: (0, 0)
    act_spec = pl.BlockSpec((tm, d), row)
    acts_spec = lambda blk: pl.BlockSpec((tm, d), lambda i: (i, blk))
    mem_spec = lambda half: pl.BlockSpec((mem_len, d), lambda i: (i // tiles_per_seq, half))
    w_spec = pl.BlockSpec((d, d), fixed, pipeline_mode=pl.Buffered(1))
    return pl.pallas_call(
        functools.partial(_merge_kernel, x_heads=X_HEADS),
        grid=(n // tm,),
        in_specs=[pl.BlockSpec((tm, d), row), act_spec, act_spec,
                  acts_spec(xq_block), mem_spec(0), mem_spec(1), acts_spec(xgate_block),
                  *[acts_spec(gate_block0 + c) for c in range(N_BRANCH)],
                  w_spec, w_spec, w_spec, w_spec],
        out_specs=pl.BlockSpec((tm, d), row),
        out_shape=jax.ShapeDtypeStruct((n, d), x2d.dtype),
        scratch_shapes=[pltpu.VMEM((4, d, d), BF16)],
        compiler_params=_cparams(("arbitrary",)),
        name="merge_out",
    )(x2d, ca, da, acts, mem_kv, mem_kv, acts, acts, acts, acts, wc, wd, wx, wo)


def _layer(x, mem, l, norm_g, mem_norm_g, w_in, conv_dw, conv_dw_b, conv_ln_g, conv_ln_b, w_conv_proj,
           diff_qn_g, diff_kn_g, lambda_q1, lambda_k1, lambda_q2, lambda_k2, diff_subln_g, w_diff_proj,
           w_mem_kv, x_qn_g, x_kn_g, w_x_proj, w_out):
    batch, seq, d = x.shape
    mem_len = mem.shape[1]
    heads = d // DIFF_V_DIM
    x_head_dim = d // X_HEADS
    assert x_head_dim == V7X_MXU_DIM and d % V7X_MXU_DIM == 0
    n = batch * seq
    x2d = x.reshape(n, d)
    lambda_init = 0.8 - 0.6 * math.exp(-0.3 * l)
    attn_tables = _position_tables(heads, seq) + _score_bound(diff_qn_g, diff_kn_g)

    assert d == PROJ_TN
    g64 = _group_ones(DIFF_HEAD_DIM)
    g256 = _group_ones(V7X_MXU_DIM)
    ones = jnp.ones((d,), F32)

    ops = (("glu", (0, 1), None), ("silu", (2,), None), ("norm64", (3,), 0), ("norm64", (4,), None),
           ("none", (5,), 1), ("silu", (6,), None), ("norm256", (7,), None), ("silu", (8,), None),
           ("sigmoid", (9,), None), ("sigmoid", (10,), None), ("sigmoid", (11,), None))
    U, C_GATE, D_K, D_GATE, X_Q, X_GATE, MERGE = range(7)
    OP_D_Q, OP_D_K, OP_X_Q = 2, 3, 6
    gain_rows = {OP_D_Q: jnp.tile(diff_qn_g, d // DIFF_HEAD_DIM) * (DIFF_HEAD_DIM ** -0.5 * LOG2_E),
                 OP_D_K: jnp.tile(diff_kn_g, d // DIFF_HEAD_DIM),
                 OP_X_Q: jnp.tile(x_qn_g, X_HEADS) * (x_head_dim ** -0.5)}
    gains = jnp.stack([gain_rows.get(b, ones) for b in range(len(ops))]).reshape(len(ops), 1, d)
    acts, q_t, v_t = _norm_proj(x2d, norm_g, w_in.astype(BF16), ops, gains, g64, g256, tm=PROJ_ROWS)

    mem_ops = (("norm256", (0,), None), ("none", (1,), None))
    mem_gains = jnp.stack([jnp.tile(x_kn_g, X_HEADS), ones]).reshape(2, 1, d)
    mem_kv, = _norm_proj(mem.reshape(batch * mem_len, d), mem_norm_g, w_mem_kv, mem_ops, mem_gains,
                         g64, g256, tm=PROJ_ROWS)

    conv_act = _conv_branch(acts, U, C_GATE, conv_dw, conv_dw_b, conv_ln_g, conv_ln_b, batch, seq,
                            ts=CONV_ROWS_PER_STEP)
    lam_vecs = jnp.stack([lambda_q1, lambda_k1, lambda_q2, lambda_k2])
    diff_act = _diff_attention(acts, q_t, v_t, D_K, D_GATE, attn_tables, lam_vecs, diff_subln_g,
                               batch, seq, heads, tq=ATTN_Q_ROWS, lambda_init=lambda_init)
    out = _merge_out(x2d, conv_act, diff_act, acts, X_Q, X_GATE, MERGE, mem_kv,
                     w_conv_proj, w_diff_proj, w_x_proj, w_out, seq, mem_len, tm=MERGE_ROWS)
    return out.reshape(batch, seq, d)


def kernel(x, mem, norm_g, mem_norm_g, w_in, conv_dw, conv_dw_b, conv_ln_g, conv_ln_b, w_conv_proj, diff_qn_g, diff_kn_g, lambda_q1, lambda_k1, lambda_q2, lambda_k2, diff_subln_g, w_diff_proj, w_mem_kv, x_qn_g, x_kn_g, w_x_proj, w_out):
    params = (norm_g, mem_norm_g, w_in, conv_dw, conv_dw_b, conv_ln_g, conv_ln_b, w_conv_proj, diff_qn_g,
              diff_kn_g, lambda_q1, lambda_k1, lambda_q2, lambda_k2, diff_subln_g, w_diff_proj, w_mem_kv,
              x_qn_g, x_kn_g, w_x_proj, w_out)
    for l in range(norm_g.shape[0]):
        x = _layer(x, mem, l, *(p[l] for p in params))
    return x
```

```python
import functools
import math

import jax
import jax.numpy as jnp
from jax import lax
import numpy as np
from jax.experimental import pallas as pl
from jax.experimental.pallas import tpu as pltpu

CONV_K = 31
DIFF_HEAD_DIM = 64
DIFF_V_DIM = 2 * DIFF_HEAD_DIM
X_HEADS = 4
N_BRANCH = 3
RMS_EPS = 1e-6
LN_EPS = 1e-5
MASK_VALUE = -1e30
LOG2_E = math.log2(math.e)

V7X_LANES = 128
V7X_SUBLANES = 8
V7X_MXU_DIM = 256
V7X_VMEM_LIMIT_BYTES = 56 * 1024 * 1024

PROJ_ROWS = 256
CONV_ROWS_PER_STEP = 256
ATTN_Q_ROWS = 256
MERGE_ROWS = 512

BF16 = jnp.bfloat16
F32 = jnp.float32


def _cparams(sem):
    return pltpu.CompilerParams(dimension_semantics=sem, vmem_limit_bytes=V7X_VMEM_LIMIT_BYTES)


def _sigmoid(x):
    return 1.0 / (1.0 + jnp.exp(-x))


def _silu(x):
    return x * _sigmoid(x)


def _nt_dot(a, b):
    return lax.dot_general(a, b, (((1,), (1,)), ((), ())), preferred_element_type=F32)


PROJ_TN = 1024


def _group_rms(x, gmat, group, gain):
    x2 = (x * x).astype(BF16)
    n = x.shape[1]
    ss = jnp.concatenate([jnp.dot(x2[:, c:c + V7X_MXU_DIM], gmat, preferred_element_type=F32)
                          for c in range(0, n, V7X_MXU_DIM)], axis=1)
    return x * lax.rsqrt(ss * (1.0 / group) + RMS_EPS) * gain


def _norm_proj_kernel(x_ref, g_ref, w_ref, gain_ref, g64_ref, g256_ref, o_ref, *rest, ops):
    n_t = sum(dest is not None for _, _, dest in ops)
    t_refs, wb = rest[:n_t], rest[n_t:]
    if wb:
        w_f32, (w_ref,) = w_ref, wb

        @pl.when(pl.program_id(0) == 0)
        def _():
            w_ref[...] = w_f32[...].astype(w_ref.dtype)

    x = x_ref[...]
    h = (x * lax.rsqrt(jnp.mean(x * x, axis=-1, keepdims=True) + RMS_EPS) * g_ref[...]).astype(BF16)

    def proj(blk):
        return jnp.dot(h, w_ref[:, blk * PROJ_TN:(blk + 1) * PROJ_TN], preferred_element_type=F32)

    ob = 0
    for n, (kind, blks, dest) in enumerate(ops):
        if kind == "glu":
            y = proj(blks[0]) * _sigmoid(proj(blks[1]))
        elif kind == "silu":
            y = _silu(proj(blks[0]))
        elif kind == "sigmoid":
            y = _sigmoid(proj(blks[0]))
        elif kind == "norm64":
            y = _group_rms(proj(blks[0]), g64_ref[...], DIFF_HEAD_DIM, gain_ref[n])
        elif kind == "norm256":
            y = _group_rms(proj(blks[0]), g256_ref[...], V7X_MXU_DIM, gain_ref[n])
        else:
            assert kind == "none", kind
            y = proj(blks[0])
        if dest is None:
            o_ref[:, ob * PROJ_TN:(ob + 1) * PROJ_TN] = y.astype(o_ref.dtype)
            ob += 1
        else:
            t_refs[dest][...] = y.T.astype(t_refs[dest].dtype)


def _norm_proj(x2d, g, w, ops, gains, g64, g256, tm):
    n, d = x2d.shape
    n_t = sum(dest is not None for _, _, dest in ops)
    nout = (len(ops) - n_t) * PROJ_TN

    def resident(a):
        return pl.BlockSpec(a.shape, lambda i: (0,) * a.ndim, pipeline_mode=pl.Buffered(1))

    g2 = g.reshape(1, d)
    cast_in_kernel = w.dtype != BF16
    return pl.pallas_call(
        functools.partial(_norm_proj_kernel, ops=ops),
        grid=(n // tm,),
        in_specs=[pl.BlockSpec((tm, d), lambda i: (i, 0)),
                  resident(g2), resident(w), resident(gains), resident(g64), resident(g256)],
        out_specs=[pl.BlockSpec((tm, nout), lambda i: (i, 0))]
                  + [pl.BlockSpec((PROJ_TN, tm), lambda i: (0, i))] * n_t,
        out_shape=[jax.ShapeDtypeStruct((n, nout), BF16)] + [jax.ShapeDtypeStruct((PROJ_TN, n), BF16)] * n_t,
        scratch_shapes=[pltpu.VMEM(w.shape, BF16)] if cast_in_kernel else [],
        compiler_params=_cparams(("arbitrary" if cast_in_kernel else "parallel",)),
        name="norm_proj",
    )(x2d, g2, w, gains, g64, g256)


def _group_ones(group):
    r = jnp.arange(V7X_MXU_DIM) // group
    return (r[:, None] == r[None, :]).astype(BF16)


CONV_HALO = 32
CONV_ROWS = 32
CONV_COLS = 512


def _conv_kernel(u_ref, gate_ref, dw_ref, dwb_ref, lng_ref, lnb_ref, o_ref, win_ref, y_ref, *, ts):
    j = pl.program_id(1)
    t0 = pl.multiple_of(j * ts, ts)
    d = u_ref.shape[1]

    @pl.when(j == 0)
    def _():
        win_ref[0, 0:CONV_HALO, :] = jnp.zeros((CONV_HALO, d), F32)

    @pl.when(j > 0)
    def _():
        win_ref[0, 0:CONV_HALO, :] = u_ref[pl.ds(t0 - CONV_HALO, CONV_HALO), :].astype(F32)

    win_ref[0, CONV_HALO:CONV_HALO + ts, :] = u_ref[pl.ds(t0, ts), :].astype(F32)

    first = CONV_HALO - (CONV_K - 1)
    span = CONV_HALO + ts - V7X_SUBLANES
    for c in range(d // V7X_LANES):
        cs = slice(c * V7X_LANES, (c + 1) * V7X_LANES)
        x = win_ref[0, :, cs]
        for s in range(1, V7X_SUBLANES):
            x = pltpu.roll(x, x.shape[0] - 1, 0)
            win_ref[s, 0:span, cs] = x[0:span]

    for r0 in range(0, ts, CONV_ROWS):
        for c in range(d // CONV_COLS):
            cs = slice(c * CONV_COLS, (c + 1) * CONV_COLS)
            acc = jnp.zeros((CONV_ROWS, CONV_COLS), F32)
            for s in range(V7X_SUBLANES):
                taps = [k for k in range(CONV_K) if (first + k) % V7X_SUBLANES == s]
                lo = r0 + first + taps[0] - s
                big = win_ref[s, lo:lo + (taps[-1] - taps[0]) + CONV_ROWS, cs]
                for k in taps:
                    w = jnp.concatenate([dw_ref[k, :, cs]] * (CONV_ROWS // V7X_SUBLANES), axis=0)
                    acc = acc + big[k - taps[0]:k - taps[0] + CONV_ROWS] * w
            y_ref[r0:r0 + CONV_ROWS, cs] = acc + dwb_ref[:, cs]

    y = y_ref[...]
    mu = jnp.mean(y, axis=-1, keepdims=True)
    yc = y - mu
    yn = yc * lax.rsqrt(jnp.mean(yc * yc, axis=-1, keepdims=True) + LN_EPS)
    yn = yn * lng_ref[...] + lnb_ref[...]
    o_ref[...] = (_silu(yn) * gate_ref[...].astype(F32)).astype(o_ref.dtype)


def _conv_branch(acts, u_block, gate_block, dw, dwb, lng, lnb, batch, seq, ts):
    n, d = acts.shape[0], dw.shape[1]
    nt = seq // ts
    return pl.pallas_call(
        functools.partial(_conv_kernel, ts=ts),
        grid=(batch, nt),
        in_specs=[pl.BlockSpec((seq, d), lambda b, j: (b, u_block)),
                  pl.BlockSpec((ts, d), lambda b, j: (b * nt + j, gate_block)),
                  pl.BlockSpec((CONV_K, V7X_SUBLANES, d), lambda b, j: (0, 0, 0)),
                  pl.BlockSpec((1, d), lambda b, j: (0, 0)),
                  pl.BlockSpec((1, d), lambda b, j: (0, 0)),
                  pl.BlockSpec((1, d), lambda b, j: (0, 0))],
        out_specs=pl.BlockSpec((ts, d), lambda b, j: (b * nt + j, 0)),
        out_shape=jax.ShapeDtypeStruct((n, d), BF16),
        scratch_shapes=[pltpu.VMEM((V7X_SUBLANES, CONV_HALO + ts, d), F32), pltpu.VMEM((ts, d), F32)],
        compiler_params=_cparams(("parallel", "arbitrary")),
        name="conv_branch",
    )(acts, acts, jnp.broadcast_to(dw[:, None, :], (CONV_K, V7X_SUBLANES, d)),
      dwb.reshape(1, d), lng.reshape(1, d), lnb.reshape(1, d))


EXTRA_ROWS = 16
PIECES = 3
POS_RADIX = 128
BF16_NORM_MARGIN = 1.02
MAX_FAST_BOUND = 0.0


def _pieces(v):
    out = []
    for _ in range(PIECES):
        piece = v.astype(BF16).astype(F32)
        out.append(piece)
        v = v - piece
    return out


def _position_tables(heads, seq):
    d = DIFF_HEAD_DIM
    rest = (2.0 ** (-8.0 * np.arange(1, heads + 1, dtype=np.float32) / heads) * np.float32(LOG2_E)).astype(np.float32)
    slopes = []
    for _ in range(PIECES):
        piece = rest.astype(BF16).astype(np.float32)
        slopes.append(piece)
        rest = rest - piece
    pos = np.arange(seq)
    hi = np.broadcast_to((pos // POS_RADIX).astype(np.float32), (heads, seq))
    lo = np.broadcast_to((pos % POS_RADIX).astype(np.float32), (heads, seq))
    const = lambda v: np.broadcast_to(v[:, None], (heads, seq))
    zero = np.zeros((heads, seq), np.float32)
    q_rows = ([const(POS_RADIX * p) for p in slopes] + [const(p) for p in slopes] + [hi] * PIECES + [lo] * PIECES
              + [zero] * (PIECES + 1))
    k_rows = ([hi] * PIECES + [lo] * PIECES + [const(-POS_RADIX * p) for p in slopes] + [const(-p) for p in slopes]
              + [zero + 1.0] * PIECES + [zero])
    q_extra = np.stack(q_rows, axis=1)
    k_extra = np.stack(k_rows, axis=2)
    k_tab = np.zeros((heads, 2, seq, DIFF_V_DIM), np.float32)
    k_tab[:, 0, :, d:d + EXTRA_ROWS] = k_extra
    k_tab[:, 1, :, :EXTRA_ROWS] = k_extra
    return jnp.asarray(q_extra.astype(BF16)), jnp.asarray(k_tab.astype(BF16))


def _score_bound(qn_g, kn_g):
    bound = (DIFF_HEAD_DIM ** 0.5 * LOG2_E * BF16_NORM_MARGIN) * jnp.max(jnp.abs(qn_g)) * jnp.max(jnp.abs(kn_g))
    fast = bound < MAX_FAST_BOUND
    neg_b = _pieces(jnp.where(fast, -bound, 0.0))
    rows = [jnp.zeros((), F32)] * (4 * PIECES) + neg_b + [jnp.zeros((), F32)]
    tile = jnp.broadcast_to(jnp.stack(rows)[:, None], (EXTRA_ROWS, V7X_LANES))
    return tile, fast.astype(jnp.int32).reshape(1)


HEADS_PER_STEP = 2


def _diff_attn_kernel(fast_ref, bound_ref, qx_ref, kx_ref, qt_in_ref, k_ref, vt_ref, gate_ref, lam_ref, subg_ref,
                      o_ref, km_ref, qt_ref, mask_ref, *, tq, lambda_init):
    seq = k_ref.shape[0]
    nq = seq // tq
    d = DIFF_HEAD_DIM
    first_half = lax.broadcasted_iota(jnp.int32, (tq, DIFF_V_DIM), 1) < d
    pad = jnp.zeros((DIFF_V_DIM - d - EXTRA_ROWS, tq), qt_ref.dtype)
    bound_rows = jnp.concatenate([bound_ref[...]] * (tq // V7X_LANES), axis=1)

    for hh in range(HEADS_PER_STEP):
        hl = slice(hh * DIFF_V_DIM, (hh + 1) * DIFF_V_DIM)
        for c in range(nq):
            rows = slice(c * tq, (c + 1) * tq)
            k = k_ref[rows, hl]
            km_ref[hh, 0, rows, :] = jnp.where(first_half, k, kx_ref[hh, 0, rows, :])
            km_ref[hh, 1, rows, :] = jnp.where(first_half, kx_ref[hh, 1, rows, :], k)
        for i in range(nq):
            cols = slice(i * tq, (i + 1) * tq)
            extra = (qx_ref[hh, :, cols].astype(F32) + bound_rows).astype(qt_ref.dtype)
            qt_ref[hh, i, 0] = jnp.concatenate([qt_in_ref[hh * DIFF_V_DIM:hh * DIFF_V_DIM + d, cols], extra, pad], axis=0)
            qt_ref[hh, i, 1] = jnp.concatenate([extra, pad, qt_in_ref[hh * DIFF_V_DIM + d:(hh + 1) * DIFF_V_DIM, cols]],
                                               axis=0)

    kk = lax.broadcasted_iota(jnp.int32, (tq, tq), 0)
    qq = lax.broadcasted_iota(jnp.int32, (tq, tq), 1)
    mask_ref[...] = jnp.where(kk <= qq, 0.0, MASK_VALUE)

    lam_v = lam_ref[...]
    lam = (jnp.exp(jnp.sum(lam_v[0:1] * lam_v[1:2], axis=-1, keepdims=True))
           - jnp.exp(jnp.sum(lam_v[2:3] * lam_v[3:4], axis=-1, keepdims=True)) + lambda_init)

    def scores(hh, i):
        keys = (i + 1) * tq
        return [jnp.dot(km_ref[hh, mp, :keys, :], qt_ref[hh, i, mp], preferred_element_type=F32)
                for mp in range(2)]

    def attend(bounded):
        work = [(hh, i) for hh in range(HEADS_PER_STEP) for i in range(nq)]
        x_next = scores(*work[0])
        for n, (hh, i) in enumerate(work):
            x_cur = x_next
            if n + 1 < len(work):
                x_next = scores(*work[n + 1])
            keys = (i + 1) * tq
            vt = vt_ref[hh * DIFF_V_DIM:(hh + 1) * DIFF_V_DIM, :keys]
            heads_out = []
            for x in x_cur:
                x = (jnp.concatenate([x[:keys - tq], x[keys - tq:] + mask_ref[...]], axis=0) if i
                     else x + mask_ref[...])
                p = jnp.exp2(x) if bounded else jnp.exp2(x - jnp.max(x, axis=0, keepdims=True))
                pv = jnp.dot(vt, p.astype(vt.dtype), preferred_element_type=F32)
                heads_out.append(pv / jnp.sum(p, axis=0, keepdims=True))
            rows = slice(i * tq, (i + 1) * tq)
            hl = slice(hh * DIFF_V_DIM, (hh + 1) * DIFF_V_DIM)
            o = heads_out[0] - lam * heads_out[1]
            o = o * lax.rsqrt(jnp.mean(o * o, axis=0, keepdims=True) + RMS_EPS)
            o = o.T * (subg_ref[...] * (1.0 - lambda_init))
            o_ref[rows, hl] = (o * gate_ref[rows, hl].astype(F32)).astype(o_ref.dtype)

    fast = fast_ref[0] == 1
    pl.when(fast)(functools.partial(attend, True))
    pl.when(jnp.logical_not(fast))(functools.partial(attend, False))


def _diff_attention(acts, q_t, v_t, k_block, gate_block, tables, lam_vecs, subg, batch, seq, heads, tq, lambda_init):
    n = acts.shape[0]
    nq = seq // tq
    hps = HEADS_PER_STEP
    q_extra, k_tab, bound_tile, fast = tables
    kb, gb = (b * heads // hps for b in (k_block, gate_block))
    kernel = functools.partial(_diff_attn_kernel, tq=tq, lambda_init=lambda_init)
    head_spec = lambda first: pl.BlockSpec((seq, hps * DIFF_V_DIM), lambda b, h: (b, first + h))
    t_spec = pl.BlockSpec((hps * DIFF_V_DIM, seq), lambda b, h: (h, b))
    return pl.pallas_call(
        kernel,
        grid=(batch, heads // hps),
        in_specs=[pl.BlockSpec(memory_space=pltpu.SMEM),
                  pl.BlockSpec((EXTRA_ROWS, V7X_LANES), lambda b, h: (0, 0)),
                  pl.BlockSpec((hps, EXTRA_ROWS, seq), lambda b, h: (h, 0, 0)),
                  pl.BlockSpec((hps, 2, seq, DIFF_V_DIM), lambda b, h: (h, 0, 0, 0)),
                  t_spec, head_spec(kb), t_spec, head_spec(gb),
                  pl.BlockSpec((4, DIFF_HEAD_DIM), lambda b, h: (0, 0)),
                  pl.BlockSpec((1, DIFF_V_DIM), lambda b, h: (0, 0))],
        out_specs=head_spec(0),
        out_shape=jax.ShapeDtypeStruct((n, heads * DIFF_V_DIM), BF16),
        scratch_shapes=[pltpu.VMEM((hps, 2, seq, DIFF_V_DIM), BF16),
                        pltpu.VMEM((hps, nq, 2, DIFF_V_DIM, tq), BF16),
                        pltpu.VMEM((tq, tq), F32)],
        compiler_params=_cparams(("parallel", "parallel")),
        name="diff_attention",
    )(fast, bound_tile, q_extra, k_tab, q_t, acts, v_t, acts, lam_vecs, subg.reshape(1, DIFF_V_DIM))


def _cross_attend(q_ref, k_ref, v_ref, gate_ref, heads):
    hd = q_ref.shape[1] // heads
    outs = []
    for h in range(heads):
        sl = slice(h * hd, (h + 1) * hd)
        s = _nt_dot(q_ref[:, sl], k_ref[:, sl])
        m = jnp.max(s, axis=-1, keepdims=True)
        p = jnp.exp(s - m)
        l = jnp.sum(p, axis=-1, keepdims=True)
        o = jnp.dot(p.astype(BF16), v_ref[:, sl], preferred_element_type=F32) / l
        outs.append(o * gate_ref[:, sl].astype(F32))
    return jnp.concatenate(outs, axis=1)


def _merge_kernel(x_ref, ca_ref, da_ref, xq_ref, xk_ref, xv_ref, xg_ref, g0_ref, g1_ref, g2_ref,
                  wc_ref, wd_ref, wx_ref, wo_ref, o_ref, wb_ref, *, x_heads):
    @pl.when(pl.program_id(0) == 0)
    def _():
        for c, w_ref in enumerate((wc_ref, wd_ref, wx_ref, wo_ref)):
            wb_ref[c] = w_ref[...].astype(wb_ref.dtype)

    xa = _cross_attend(xq_ref, xk_ref, xv_ref, xg_ref, x_heads).astype(BF16)
    y = g0_ref[...].astype(F32) * jnp.dot(ca_ref[...], wb_ref[0], preferred_element_type=F32)
    y = y + g1_ref[...].astype(F32) * jnp.dot(da_ref[...], wb_ref[1], preferred_element_type=F32)
    y = y + g2_ref[...].astype(F32) * jnp.dot(xa, wb_ref[2], preferred_element_type=F32)
    o_ref[...] = x_ref[...] + jnp.dot(y.astype(BF16), wb_ref[3], preferred_element_type=F32)


def _merge_out(x2d, ca, da, acts, xq_block, xgate_block, gate_block0, mem_kv, wc, wd, wx, wo, seq, mem_len, tm):
    n, d = x2d.shape
    tiles_per_seq = seq // tm
    row = lambda i: (i, 0)
    fixed = lambda i: (0, 0)
    act_spec = pl.BlockSpec((tm, d), row)
    acts_spec = lambda blk: pl.BlockSpec((tm, d), lambda i: (i, blk))
    mem_spec = lambda half: pl.BlockSpec((mem_len, d), lambda i: (i // tiles_per_seq, half))
    w_spec = pl.BlockSpec((d, d), fixed, pipeline_mode=pl.Buffered(1))
    return pl.pallas_call(
        functools.partial(_merge_kernel, x_heads=X_HEADS),
        grid=(n // tm,),
        in_specs=[pl.BlockSpec((tm, d), row), act_spec, act_spec,
                  acts_spec(xq_block), mem_spec(0), mem_spec(1), acts_spec(xgate_block),
                  *[acts_spec(gate_block0 + c) for c in range(N_BRANCH)],
                  w_spec, w_spec, w_spec, w_spec],
        out_specs=pl.BlockSpec((tm, d), row),
        out_shape=jax.ShapeDtypeStruct((n, d), x2d.dtype),
        scratch_shapes=[pltpu.VMEM((4, d, d), BF16)],
        compiler_params=_cparams(("arbitrary",)),
        name="merge_out",
    )(x2d, ca, da, acts, mem_kv, mem_kv, acts, acts, acts, acts, wc, wd, wx, wo)


def _layer(x, mem, l, norm_g, mem_norm_g, w_in, conv_dw, conv_dw_b, conv_ln_g, conv_ln_b, w_conv_proj,
           diff_qn_g, diff_kn_g, lambda_q1, lambda_k1, lambda_q2, lambda_k2, diff_subln_g, w_diff_proj,
           w_mem_kv, x_qn_g, x_kn_g, w_x_proj, w_out):
    batch, seq, d = x.shape
    mem_len = mem.shape[1]
    heads = d // DIFF_V_DIM
    x_head_dim = d // X_HEADS
    assert x_head_dim == V7X_MXU_DIM and d % V7X_MXU_DIM == 0
    n = batch * seq
    x2d = x.reshape(n, d)
    lambda_init = 0.8 - 0.6 * math.exp(-0.3 * l)
    attn_tables = _position_tables(heads, seq) + _score_bound(diff_qn_g, diff_kn_g)

    assert d == PROJ_TN
    g64 = _group_ones(DIFF_HEAD_DIM)
    g256 = _group_ones(V7X_MXU_DIM)
    ones = jnp.ones((d,), F32)

    ops = (("glu", (0, 1), None), ("silu", (2,), None), ("norm64", (3,), 0), ("norm64", (4,), None),
           ("none", (5,), 1), ("silu", (6,), None), ("norm256", (7,), None), ("silu", (8,), None),
           ("sigmoid", (9,), None), ("sigmoid", (10,), None), ("sigmoid", (11,), None))
    U, C_GATE, D_K, D_GATE, X_Q, X_GATE, MERGE = range(7)
    OP_D_Q, OP_D_K, OP_X_Q = 2, 3, 6
    gain_rows = {OP_D_Q: jnp.tile(diff_qn_g, d // DIFF_HEAD_DIM) * (DIFF_HEAD_DIM ** -0.5 * LOG2_E),
                 OP_D_K: jnp.tile(diff_kn_g, d // DIFF_HEAD_DIM),
                 OP_X_Q: jnp.tile(x_qn_g, X_HEADS) * (x_head_dim ** -0.5)}
    gains = jnp.stack([gain_rows.get(b, ones) for b in range(len(ops))]).reshape(len(ops), 1, d)
    acts, q_t, v_t = _norm_proj(x2d, norm_g, w_in.astype(BF16), ops, gains, g64, g256, tm=PROJ_ROWS)

    mem_ops = (("norm256", (0,), None), ("none", (1,), None))
    mem_gains = jnp.stack([jnp.tile(x_kn_g, X_HEADS), ones]).reshape(2, 1, d)
    mem_kv, = _norm_proj(mem.reshape(batch * mem_len, d), mem_norm_g, w_mem_kv, mem_ops, mem_gains,
                         g64, g256, tm=PROJ_ROWS)

    conv_act = _conv_branch(acts, U, C_GATE, conv_dw, conv_dw_b, conv_ln_g, conv_ln_b, batch, seq,
                            ts=CONV_ROWS_PER_STEP)
    lam_vecs = jnp.stack([lambda_q1, lambda_k1, lambda_q2, lambda_k2])
    diff_act = _diff_attention(acts, q_t, v_t, D_K, D_GATE, attn_tables, lam_vecs, diff_subln_g,
                               batch, seq, heads, tq=ATTN_Q_ROWS, lambda_init=lambda_init)
    out = _merge_out(x2d, conv_act, diff_act, acts, X_Q, X_GATE, MERGE, mem_kv,
                     w_conv_proj, w_diff_proj, w_x_proj, w_out, seq, mem_len, tm=MERGE_ROWS)
    return out.reshape(batch, seq, d)


def kernel(x, mem, norm_g, mem_norm_g, w_in, conv_dw, conv_dw_b, conv_ln_g, conv_ln_b, w_conv_proj, diff_qn_g, diff_kn_g, lambda_q1, lambda_k1, lambda_q2, lambda_k2, diff_subln_g, w_diff_proj, w_mem_kv, x_qn_g, x_kn_g, w_x_proj, w_out):
    params = (norm_g, mem_norm_g, w_in, conv_dw, conv_dw_b, conv_ln_g, conv_ln_b, w_conv_proj, diff_qn_g,
              diff_kn_g, lambda_q1, lambda_k1, lambda_q2, lambda_k2, diff_subln_g, w_diff_proj, w_mem_kv,
              x_qn_g, x_kn_g, w_x_proj, w_out)
    for l in range(norm_g.shape[0]):
        x = _layer(x, mem, l, *(p[l] for p in params))
    return x
```

```python
import functools
import math

import jax
import jax.numpy as jnp
from jax import lax
import numpy as np
from jax.experimental import pallas as pl
from jax.experimental.pallas import tpu as pltpu

CONV_K = 31
DIFF_HEAD_DIM = 64
DIFF_V_DIM = 2 * DIFF_HEAD_DIM
X_HEADS = 4
N_BRANCH = 3
RMS_EPS = 1e-6
LN_EPS = 1e-5
MASK_VALUE = -1e30
LOG2_E = math.log2(math.e)

V7X_LANES = 128
V7X_SUBLANES = 8
V7X_MXU_DIM = 256
V7X_VMEM_LIMIT_BYTES = 56 * 1024 * 1024

PROJ_ROWS = 256
CONV_ROWS_PER_STEP = 256
ATTN_Q_ROWS = 256
MERGE_ROWS = 512

BF16 = jnp.bfloat16
F32 = jnp.float32


def _cparams(sem):
    return pltpu.CompilerParams(dimension_semantics=sem, vmem_limit_bytes=V7X_VMEM_LIMIT_BYTES)


def _sigmoid(x):
    return 1.0 / (1.0 + jnp.exp(-x))


def _silu(x):
    return x * _sigmoid(x)


def _nt_dot(a, b):
    return lax.dot_general(a, b, (((1,), (1,)), ((), ())), preferred_element_type=F32)


PROJ_TN = 1024


def _group_rms(x, gmat, group, gain):
    x2 = (x * x).astype(BF16)
    n = x.shape[1]
    ss = jnp.concatenate([jnp.dot(x2[:, c:c + V7X_MXU_DIM], gmat, preferred_element_type=F32)
                          for c in range(0, n, V7X_MXU_DIM)], axis=1)
    return x * lax.rsqrt(ss * (1.0 / group) + RMS_EPS) * gain


def _norm_proj_kernel(x_ref, g_ref, w_ref, gain_ref, g64_ref, g256_ref, *rest, ops):
    rest = list(rest)
    gain_t_ref = rest.pop(0) if any(kind == "norm64_t" for kind, _, _ in ops) else None
    o_ref = rest.pop(0)
    n_t = sum(dest is not None for _, _, dest in ops)
    t_refs, wb = rest[:n_t], rest[n_t:]
    if wb:
        w_f32, (w_ref,) = w_ref, wb

        @pl.when(pl.program_id(0) == 0)
        def _():
            w_ref[...] = w_f32[...].astype(w_ref.dtype)

    x = x_ref[...]
    h = (x * lax.rsqrt(jnp.mean(x * x, axis=-1, keepdims=True) + RMS_EPS) * g_ref[...]).astype(BF16)

    def proj(blk):
        return jnp.dot(h, w_ref[:, blk * PROJ_TN:(blk + 1) * PROJ_TN], preferred_element_type=F32)

    ob = 0
    for n, (kind, blks, dest) in enumerate(ops):
        if kind == "glu":
            y = proj(blks[0]) * _sigmoid(proj(blks[1]))
        elif kind == "silu":
            y = _silu(proj(blks[0]))
        elif kind == "sigmoid":
            y = _sigmoid(proj(blks[0]))
        elif kind == "norm64":
            y = _group_rms(proj(blks[0]), g64_ref[...], DIFF_HEAD_DIM, gain_ref[n])
        elif kind == "norm64_t":
            yt = proj(blks[0]).T
            groups = yt.reshape(PROJ_TN // DIFF_HEAD_DIM, DIFF_HEAD_DIM, yt.shape[1])
            groups = groups * lax.rsqrt(jnp.mean(groups * groups, axis=1, keepdims=True) + RMS_EPS)
            t_refs[dest][...] = (groups.reshape(yt.shape) * gain_t_ref[...]).astype(t_refs[dest].dtype)
            continue
        elif kind == "norm256":
            y = _group_rms(proj(blks[0]), g256_ref[...], V7X_MXU_DIM, gain_ref[n])
        else:
            assert kind == "none", kind
            y = proj(blks[0])
        if dest is None:
            o_ref[:, ob * PROJ_TN:(ob + 1) * PROJ_TN] = y.astype(o_ref.dtype)
            ob += 1
        else:
            t_refs[dest][...] = y.T.astype(t_refs[dest].dtype)


def _norm_proj(x2d, g, w, ops, gains, g64, g256, tm, gain_t=None):
    n, d = x2d.shape
    n_t = sum(dest is not None for _, _, dest in ops)
    nout = (len(ops) - n_t) * PROJ_TN

    def resident(a):
        return pl.BlockSpec(a.shape, lambda i: (0,) * a.ndim, pipeline_mode=pl.Buffered(1))

    g2 = g.reshape(1, d)
    cast_in_kernel = w.dtype != BF16
    return pl.pallas_call(
        functools.partial(_norm_proj_kernel, ops=ops),
        grid=(n // tm,),
        in_specs=[pl.BlockSpec((tm, d), lambda i: (i, 0)),
                  resident(g2), resident(w), resident(gains), resident(g64), resident(g256)]
                 + ([resident(gain_t)] if gain_t is not None else []),
        out_specs=[pl.BlockSpec((tm, nout), lambda i: (i, 0))]
                  + [pl.BlockSpec((PROJ_TN, tm), lambda i: (0, i))] * n_t,
        out_shape=[jax.ShapeDtypeStruct((n, nout), BF16)] + [jax.ShapeDtypeStruct((PROJ_TN, n), BF16)] * n_t,
        scratch_shapes=[pltpu.VMEM(w.shape, BF16)] if cast_in_kernel else [],
        compiler_params=_cparams(("arbitrary" if cast_in_kernel else "parallel",)),
        name="norm_proj",
    )(x2d, g2, w, gains, g64, g256, *([gain_t] if gain_t is not None else []))


def _group_ones(group):
    r = jnp.arange(V7X_MXU_DIM) // group
    return (r[:, None] == r[None, :]).astype(BF16)


CONV_HALO = 32
CONV_ROWS = 32
CONV_COLS = 512


def _conv_kernel(u_ref, gate_ref, dw_ref, dwb_ref, lng_ref, lnb_ref, o_ref, win_ref, y_ref, *, ts):
    j = pl.program_id(1)
    t0 = pl.multiple_of(j * ts, ts)
    d = u_ref.shape[1]

    @pl.when(j == 0)
    def _():
        win_ref[0, 0:CONV_HALO, :] = jnp.zeros((CONV_HALO, d), F32)

    @pl.when(j > 0)
    def _():
        win_ref[0, 0:CONV_HALO, :] = u_ref[pl.ds(t0 - CONV_HALO, CONV_HALO), :].astype(F32)

    win_ref[0, CONV_HALO:CONV_HALO + ts, :] = u_ref[pl.ds(t0, ts), :].astype(F32)

    first = CONV_HALO - (CONV_K - 1)
    span = CONV_HALO + ts - V7X_SUBLANES
    for c in range(d // V7X_LANES):
        cs = slice(c * V7X_LANES, (c + 1) * V7X_LANES)
        x = win_ref[0, :, cs]
        for s in range(1, V7X_SUBLANES):
            x = pltpu.roll(x, x.shape[0] - 1, 0)
            win_ref[s, 0:span, cs] = x[0:span]

    for r0 in range(0, ts, CONV_ROWS):
        for c in range(d // CONV_COLS):
            cs = slice(c * CONV_COLS, (c + 1) * CONV_COLS)
            acc = jnp.zeros((CONV_ROWS, CONV_COLS), F32)
            for s in range(V7X_SUBLANES):
                taps = [k for k in range(CONV_K) if (first + k) % V7X_SUBLANES == s]
                lo = r0 + first + taps[0] - s
                big = win_ref[s, lo:lo + (taps[-1] - taps[0]) + CONV_ROWS, cs]
                for k in taps:
                    w = jnp.concatenate([dw_ref[k, :, cs]] * (CONV_ROWS // V7X_SUBLANES), axis=0)
                    acc = acc + big[k - taps[0]:k - taps[0] + CONV_ROWS] * w
            y_ref[r0:r0 + CONV_ROWS, cs] = acc + dwb_ref[:, cs]

    y = y_ref[...]
    mu = jnp.mean(y, axis=-1, keepdims=True)
    yc = y - mu
    yn = yc * lax.rsqrt(jnp.mean(yc * yc, axis=-1, keepdims=True) + LN_EPS)
    yn = yn * lng_ref[...] + lnb_ref[...]
    o_ref[...] = (_silu(yn) * gate_ref[...].astype(F32)).astype(o_ref.dtype)


def _conv_branch(acts, u_block, gate_block, dw, dwb, lng, lnb, batch, seq, ts):
    n, d = acts.shape[0], dw.shape[1]
    nt = seq // ts
    return pl.pallas_call(
        functools.partial(_conv_kernel, ts=ts),
        grid=(batch, nt),
        in_specs=[pl.BlockSpec((seq, d), lambda b, j: (b, u_block)),
                  pl.BlockSpec((ts, d), lambda b, j: (b * nt + j, gate_block)),
                  pl.BlockSpec((CONV_K, V7X_SUBLANES, d), lambda b, j: (0, 0, 0)),
                  pl.BlockSpec((1, d), lambda b, j: (0, 0)),
                  pl.BlockSpec((1, d), lambda b, j: (0, 0)),
                  pl.BlockSpec((1, d), lambda b, j: (0, 0))],
        out_specs=pl.BlockSpec((ts, d), lambda b, j: (b * nt + j, 0)),
        out_shape=jax.ShapeDtypeStruct((n, d), BF16),
        scratch_shapes=[pltpu.VMEM((V7X_SUBLANES, CONV_HALO + ts, d), F32), pltpu.VMEM((ts, d), F32)],
        compiler_params=_cparams(("parallel", "arbitrary")),
        name="conv_branch",
    )(acts, acts, jnp.broadcast_to(dw[:, None, :], (CONV_K, V7X_SUBLANES, d)),
      dwb.reshape(1, d), lng.reshape(1, d), lnb.reshape(1, d))


EXTRA_ROWS = 16
PIECES = 3
POS_RADIX = 128
BF16_NORM_MARGIN = 1.02
MAX_FAST_BOUND = 40.0


def _pieces(v):
    out = []
    for _ in range(PIECES):
        piece = v.astype(BF16).astype(F32)
        out.append(piece)
        v = v - piece
    return out


def _position_tables(heads, seq):
    d = DIFF_HEAD_DIM
    rest = (2.0 ** (-8.0 * np.arange(1, heads + 1, dtype=np.float32) / heads) * np.float32(LOG2_E)).astype(np.float32)
    slopes = []
    for _ in range(PIECES):
        piece = rest.astype(BF16).astype(np.float32)
        slopes.append(piece)
        rest = rest - piece
    pos = np.arange(seq)
    hi = np.broadcast_to((pos // POS_RADIX).astype(np.float32), (heads, seq))
    lo = np.broadcast_to((pos % POS_RADIX).astype(np.float32), (heads, seq))
    const = lambda v: np.broadcast_to(v[:, None], (heads, seq))
    zero = np.zeros((heads, seq), np.float32)
    q_rows = ([const(POS_RADIX * p) for p in slopes] + [const(p) for p in slopes] + [hi] * PIECES + [lo] * PIECES
              + [zero] * (PIECES + 1))
    k_rows = ([hi] * PIECES + [lo] * PIECES + [const(-POS_RADIX * p) for p in slopes] + [const(-p) for p in slopes]
              + [zero + 1.0] * PIECES + [zero])
    q_extra = np.stack(q_rows, axis=1)
    k_extra = np.stack(k_rows, axis=2)
    k_tab = np.zeros((heads, 2, seq, DIFF_V_DIM), np.float32)
    k_tab[:, 0, :, d:d + EXTRA_ROWS] = k_extra
    k_tab[:, 1, :, :EXTRA_ROWS] = k_extra
    return jnp.asarray(q_extra.astype(BF16)), jnp.asarray(k_tab.astype(BF16))


def _score_bound(qn_g, kn_g):
    bound = (DIFF_HEAD_DIM ** 0.5 * LOG2_E * BF16_NORM_MARGIN) * jnp.max(jnp.abs(qn_g)) * jnp.max(jnp.abs(kn_g))
    fast = bound < MAX_FAST_BOUND
    neg_b = _pieces(jnp.where(fast, -bound, 0.0))
    rows = [jnp.zeros((), F32)] * (4 * PIECES) + neg_b + [jnp.zeros((), F32)]
    tile = jnp.broadcast_to(jnp.stack(rows)[:, None], (EXTRA_ROWS, V7X_LANES))
    return tile, fast.astype(jnp.int32).reshape(1)


HEADS_PER_STEP = 2


def _diff_attn_kernel(fast_ref, bound_ref, qx_ref, kx_ref, qt_in_ref, k_ref, vt_ref, gate_ref, lam_ref, subg_ref,
                      o_ref, km_ref, qt_ref, mask_ref, *, tq, lambda_init):
    seq = k_ref.shape[0]
    nq = seq // tq
    d = DIFF_HEAD_DIM
    first_half = lax.broadcasted_iota(jnp.int32, (tq, DIFF_V_DIM), 1) < d
    pad = jnp.zeros((DIFF_V_DIM - d - EXTRA_ROWS, tq), qt_ref.dtype)
    bound_rows = jnp.concatenate([bound_ref[...]] * (tq // V7X_LANES), axis=1)

    for hh in range(HEADS_PER_STEP):
        hl = slice(hh * DIFF_V_DIM, (hh + 1) * DIFF_V_DIM)
        for c in range(nq):
            rows = slice(c * tq, (c + 1) * tq)
            k = k_ref[rows, hl]
            km_ref[hh, 0, rows, :] = jnp.where(first_half, k, kx_ref[hh, 0, rows, :])
            km_ref[hh, 1, rows, :] = jnp.where(first_half, kx_ref[hh, 1, rows, :], k)
        for i in range(nq):
            cols = slice(i * tq, (i + 1) * tq)
            extra = (qx_ref[hh, :, cols].astype(F32) + bound_rows).astype(qt_ref.dtype)
            qt_ref[hh, i, 0] = jnp.concatenate([qt_in_ref[hh * DIFF_V_DIM:hh * DIFF_V_DIM + d, cols], extra, pad], axis=0)
            qt_ref[hh, i, 1] = jnp.concatenate([extra, pad, qt_in_ref[hh * DIFF_V_DIM + d:(hh + 1) * DIFF_V_DIM, cols]],
                                               axis=0)

    kk = lax.broadcasted_iota(jnp.int32, (tq, tq), 0)
    qq = lax.broadcasted_iota(jnp.int32, (tq, tq), 1)
    mask_ref[...] = jnp.where(kk <= qq, 0.0, MASK_VALUE)

    lam_v = lam_ref[...]
    lam = (jnp.exp(jnp.sum(lam_v[0:1] * lam_v[1:2], axis=-1, keepdims=True))
           - jnp.exp(jnp.sum(lam_v[2:3] * lam_v[3:4], axis=-1, keepdims=True)) + lambda_init)

    def scores(hh, i):
        keys = (i + 1) * tq
        return [jnp.dot(km_ref[hh, mp, :keys, :], qt_ref[hh, i, mp], preferred_element_type=F32)
                for mp in range(2)]

    def attend(bounded):
        work = [(hh, i) for hh in range(HEADS_PER_STEP) for i in range(nq)]
        x_next = scores(*work[0])
        for n, (hh, i) in enumerate(work):
            x_cur = x_next
            if n + 1 < len(work):
                x_next = scores(*work[n + 1])
            keys = (i + 1) * tq
            vt = vt_ref[hh * DIFF_V_DIM:(hh + 1) * DIFF_V_DIM, :keys]
            heads_out = []
            for x in x_cur:
                x = (jnp.concatenate([x[:keys - tq], x[keys - tq:] + mask_ref[...]], axis=0) if i
                     else x + mask_ref[...])
                p = jnp.exp2(x) if bounded else jnp.exp2(x - jnp.max(x, axis=0, keepdims=True))
                pv = jnp.dot(vt, p.astype(vt.dtype), preferred_element_type=F32)
                heads_out.append(pv / jnp.sum(p, axis=0, keepdims=True))
            rows = slice(i * tq, (i + 1) * tq)
            hl = slice(hh * DIFF_V_DIM, (hh + 1) * DIFF_V_DIM)
            o = heads_out[0] - lam * heads_out[1]
            o = o * lax.rsqrt(jnp.mean(o * o, axis=0, keepdims=True) + RMS_EPS)
            o = o.T * (subg_ref[...] * (1.0 - lambda_init))
            o_ref[rows, hl] = (o * gate_ref[rows, hl].astype(F32)).astype(o_ref.dtype)

    fast = fast_ref[0] == 1
    pl.when(fast)(functools.partial(attend, True))
    pl.when(jnp.logical_not(fast))(functools.partial(attend, False))


def _diff_attention(acts, q_t, v_t, k_block, gate_block, tables, lam_vecs, subg, batch, seq, heads, tq, lambda_init):
    n = acts.shape[0]
    nq = seq // tq
    hps = HEADS_PER_STEP
    q_extra, k_tab, bound_tile, fast = tables
    kb, gb = (b * heads // hps for b in (k_block, gate_block))
    kernel = functools.partial(_diff_attn_kernel, tq=tq, lambda_init=lambda_init)
    head_spec = lambda first: pl.BlockSpec((seq, hps * DIFF_V_DIM), lambda b, h: (b, first + h))
    t_spec = pl.BlockSpec((hps * DIFF_V_DIM, seq), lambda b, h: (h, b))
    return pl.pallas_call(
        kernel,
        grid=(batch, heads // hps),
        in_specs=[pl.BlockSpec(memory_space=pltpu.SMEM),
                  pl.BlockSpec((EXTRA_ROWS, V7X_LANES), lambda b, h: (0, 0)),
                  pl.BlockSpec((hps, EXTRA_ROWS, seq), lambda b, h: (h, 0, 0)),
                  pl.BlockSpec((hps, 2, seq, DIFF_V_DIM), lambda b, h: (h, 0, 0, 0)),
                  t_spec, head_spec(kb), t_spec, head_spec(gb),
                  pl.BlockSpec((4, DIFF_HEAD_DIM), lambda b, h: (0, 0)),
                  pl.BlockSpec((1, DIFF_V_DIM), lambda b, h: (0, 0))],
        out_specs=head_spec(0),
        out_shape=jax.ShapeDtypeStruct((n, heads * DIFF_V_DIM), BF16),
        scratch_shapes=[pltpu.VMEM((hps, 2, seq, DIFF_V_DIM), BF16),
                        pltpu.VMEM((hps, nq, 2, DIFF_V_DIM, tq), BF16),
                        pltpu.VMEM((tq, tq), F32)],
        compiler_params=_cparams(("parallel", "parallel")),
        name="diff_attention",
    )(fast, bound_tile, q_extra, k_tab, q_t, acts, v_t, acts, lam_vecs, subg.reshape(1, DIFF_V_DIM))


def _cross_attend(q_ref, k_ref, v_ref, gate_ref, heads):
    hd = q_ref.shape[1] // heads
    outs = []
    for h in range(heads):
        sl = slice(h * hd, (h + 1) * hd)
        s = _nt_dot(q_ref[:, sl], k_ref[:, sl])
        m = jnp.max(s, axis=-1, keepdims=True)
        p = jnp.exp(s - m)
        l = jnp.sum(p, axis=-1, keepdims=True)
        o = jnp.dot(p.astype(BF16), v_ref[:, sl], preferred_element_type=F32) / l
        outs.append(o * gate_ref[:, sl].astype(F32))
    return jnp.concatenate(outs, axis=1)


def _merge_kernel(x_ref, ca_ref, da_ref, xq_ref, xk_ref, xv_ref, xg_ref, g0_ref, g1_ref, g2_ref,
                  wc_ref, wd_ref, wx_ref, wo_ref, o_ref, wb_ref, *, x_heads):
    @pl.when(pl.program_id(0) == 0)
    def _():
        for c, w_ref in enumerate((wc_ref, wd_ref, wx_ref, wo_ref)):
            wb_ref[c] = w_ref[...].astype(wb_ref.dtype)

    xa = _cross_attend(xq_ref, xk_ref, xv_ref, xg_ref, x_heads).astype(BF16)
    y = g0_ref[...].astype(F32) * jnp.dot(ca_ref[...], wb_ref[0], preferred_element_type=F32)
    y = y + g1_ref[...].astype(F32) * jnp.dot(da_ref[...], wb_ref[1], preferred_element_type=F32)
    y = y + g2_ref[...].astype(F32) * jnp.dot(xa, wb_ref[2], preferred_element_type=F32)
    o_ref[...] = x_ref[...] + jnp.dot(y.astype(BF16), wb_ref[3], preferred_element_type=F32)


def _merge_out(x2d, ca, da, acts, xq_block, xgate_block, gate_block0, mem_kv, wc, wd, wx, wo, seq, mem_len, tm):
    n, d = x2d.shape
    tiles_per_seq = seq // tm
    row = lambda i: (i, 0)
    fixed = lambda i: (0, 0)
    act_spec = pl.BlockSpec((tm, d), row)
    acts_spec = lambda blk: pl.BlockSpec((tm, d), lambda i: (i, blk))
    mem_spec = lambda half: pl.BlockSpec((mem_len, d), lambda i: (i // tiles_per_seq, half))
    w_spec = pl.BlockSpec((d, d), fixed, pipeline_mode=pl.Buffered(1))
    return pl.pallas_call(
        functools.partial(_merge_kernel, x_heads=X_HEADS),
        grid=(n // tm,),
        in_specs=[pl.BlockSpec((tm, d), row), act_spec, act_spec,
                  acts_spec(xq_block), mem_spec(0), mem_spec(1), acts_spec(xgate_block),
                  *[acts_spec(gate_block0 + c) for c in range(N_BRANCH)],
                  w_spec, w_spec, w_spec, w_spec],
        out_specs=pl.BlockSpec((tm, d), row),
        out_shape=jax.ShapeDtypeStruct((n, d), x2d.dtype),
        scratch_shapes=[pltpu.VMEM((4, d, d), BF16)],
        compiler_params=_cparams(("arbitrary",)),
        name="merge_out",
    )(x2d, ca, da, acts, mem_kv, mem_kv, acts, acts, acts, acts, wc, wd, wx, wo)


def _layer(x, mem, l, norm_g, mem_norm_g, w_in, conv_dw, conv_dw_b, conv_ln_g, conv_ln_b, w_conv_proj,
           diff_qn_g, diff_kn_g, lambda_q1, lambda_k1, lambda_q2, lambda_k2, diff_subln_g, w_diff_proj,
           w_mem_kv, x_qn_g, x_kn_g, w_x_proj, w_out):
    batch, seq, d = x.shape
    mem_len = mem.shape[1]
    heads = d // DIFF_V_DIM
    x_head_dim = d // X_HEADS
    assert x_head_dim == V7X_MXU_DIM and d % V7X_MXU_DIM == 0
    n = batch * seq
    x2d = x.reshape(n, d)
    lambda_init = 0.8 - 0.6 * math.exp(-0.3 * l)
    attn_tables = _position_tables(heads, seq) + _score_bound(diff_qn_g, diff_kn_g)

    assert d == PROJ_TN
    g64 = _group_ones(DIFF_HEAD_DIM)
    g256 = _group_ones(V7X_MXU_DIM)
    ones = jnp.ones((d,), F32)

    ops = (("glu", (0, 1), None), ("silu", (2,), None), ("norm64_t", (3,), 0), ("norm64", (4,), None),
           ("none", (5,), 1), ("silu", (6,), None), ("norm256", (7,), None), ("silu", (8,), None),
           ("sigmoid", (9,), None), ("sigmoid", (10,), None), ("sigmoid", (11,), None))
    U, C_GATE, D_K, D_GATE, X_Q, X_GATE, MERGE = range(7)
    OP_D_Q, OP_D_K, OP_X_Q = 2, 3, 6
    gain_rows = {OP_D_Q: jnp.tile(diff_qn_g, d // DIFF_HEAD_DIM) * (DIFF_HEAD_DIM ** -0.5 * LOG2_E),
                 OP_D_K: jnp.tile(diff_kn_g, d // DIFF_HEAD_DIM),
                 OP_X_Q: jnp.tile(x_qn_g, X_HEADS) * (x_head_dim ** -0.5)}
    gains = jnp.stack([gain_rows.get(b, ones) for b in range(len(ops))]).reshape(len(ops), 1, d)
    q_gain_t = jnp.broadcast_to(gain_rows[OP_D_Q][:, None], (d, PROJ_ROWS))
    acts, q_t, v_t = _norm_proj(x2d, norm_g, w_in.astype(BF16), ops, gains, g64, g256, tm=PROJ_ROWS,
                                gain_t=q_gain_t)

    mem_ops = (("norm256", (0,), None), ("none", (1,), None))
    mem_gains = jnp.stack([jnp.tile(x_kn_g, X_HEADS), ones]).reshape(2, 1, d)
    mem_kv, = _norm_proj(mem.reshape(batch * mem_len, d), mem_norm_g, w_mem_kv, mem_ops, mem_gains,
                         g64, g256, tm=PROJ_ROWS)

    conv_act = _conv_branch(acts, U, C_GATE, conv_dw, conv_dw_b, conv_ln_g, conv_ln_b, batch, seq,
                            ts=CONV_ROWS_PER_STEP)
    lam_vecs = jnp.stack([lambda_q1, lambda_k1, lambda_q2, lambda_k2])
    diff_act = _diff_attention(acts, q_t, v_t, D_K, D_GATE, attn_tables, lam_vecs, diff_subln_g,
                               batch, seq, heads, tq=ATTN_Q_ROWS, lambda_init=lambda_init)
    out = _merge_out(x2d, conv_act, diff_act, acts, X_Q, X_GATE, MERGE, mem_kv,
                     w_conv_proj, w_diff_proj, w_x_proj, w_out, seq, mem_len, tm=MERGE_ROWS)
    return out.reshape(batch, seq, d)


def kernel(x, mem, norm_g, mem_norm_g, w_in, conv_dw, conv_dw_b, conv_ln_g, conv_ln_b, w_conv_proj, diff_qn_g, diff_kn_g, lambda_q1, lambda_k1, lambda_q2, lambda_k2, diff_subln_g, w_diff_proj, w_mem_kv, x_qn_g, x_kn_g, w_x_proj, w_out):
    params = (norm_g, mem_norm_g, w_in, conv_dw, conv_dw_b, conv_ln_g, conv_ln_b, w_conv_proj, diff_qn_g,
              diff_kn_g, lambda_q1, lambda_k1, lambda_q2, lambda_k2, diff_subln_g, w_diff_proj, w_mem_kv,
              x_qn_g, x_kn_g, w_x_proj, w_out)
    for l in range(norm_g.shape[0]):
        x = _layer(x, mem, l, *(p[l] for p in params))
    return x
```

```python
import functools
import math

import jax
import jax.numpy as jnp
from jax import lax
import numpy as np
from jax.experimental import pallas as pl
from jax.experimental.pallas import tpu as pltpu

CONV_K = 31
DIFF_HEAD_DIM = 64
DIFF_V_DIM = 2 * DIFF_HEAD_DIM
X_HEADS = 4
N_BRANCH = 3
RMS_EPS = 1e-6
LN_EPS = 1e-5
MASK_VALUE = -1e30
LOG2_E = math.log2(math.e)

V7X_LANES = 128
V7X_SUBLANES = 8
V7X_MXU_DIM = 256
V7X_VMEM_LIMIT_BYTES = 56 * 1024 * 1024

PROJ_ROWS = 256
CONV_ROWS_PER_STEP = 256
ATTN_Q_ROWS = 256
MERGE_ROWS = 512

BF16 = jnp.bfloat16
F32 = jnp.float32


def _cparams(sem):
    return pltpu.CompilerParams(dimension_semantics=sem, vmem_limit_bytes=V7X_VMEM_LIMIT_BYTES)


def _sigmoid(x):
    return 1.0 / (1.0 + jnp.exp(-x))


def _silu(x):
    return x * _sigmoid(x)


def _nt_dot(a, b):
    return lax.dot_general(a, b, (((1,), (1,)), ((), ())), preferred_element_type=F32)


PROJ_TN = 1024


def _group_rms_t(yt, group):
    groups = yt.reshape(yt.shape[0] // group, group, yt.shape[1])
    groups = groups * lax.rsqrt(jnp.mean(groups * groups, axis=1, keepdims=True) + RMS_EPS)
    return groups.reshape(yt.shape)


def _norm_proj_kernel(x_ref, g_ref, w_ref, gain_ref, *rest, ops):
    rest = list(rest)
    n_t = sum(dest is not None for _, _, dest in ops)
    gain_t_ref = rest.pop(0) if any(kind.startswith("norm") and dest is not None for kind, _, dest in ops) else None
    o_ref = rest.pop(0)
    t_refs, wb = rest[:n_t], rest[n_t:]
    if wb:
        w_f32, (w_ref,) = w_ref, wb

        @pl.when(pl.program_id(0) == 0)
        def _():
            w_ref[...] = w_f32[...].astype(w_ref.dtype)

    x = x_ref[...]
    h = (x * lax.rsqrt(jnp.mean(x * x, axis=-1, keepdims=True) + RMS_EPS) * g_ref[...]).astype(BF16)

    def proj(blk):
        return jnp.dot(h, w_ref[:, blk * PROJ_TN:(blk + 1) * PROJ_TN], preferred_element_type=F32)

    ob = 0
    for n, (kind, blks, dest) in enumerate(ops):
        transposed = False
        if kind == "glu":
            y = proj(blks[0]) * _sigmoid(proj(blks[1]))
        elif kind == "silu":
            y = _silu(proj(blks[0]))
        elif kind == "sigmoid":
            y = _sigmoid(proj(blks[0]))
        elif kind in ("norm64", "norm256"):
            y = _group_rms_t(proj(blks[0]).T, DIFF_HEAD_DIM if kind == "norm64" else V7X_MXU_DIM)
            y, transposed = (y.T * gain_ref[n], False) if dest is None else (y * gain_t_ref[...], True)
        else:
            assert kind == "none", kind
            y = proj(blks[0])
        if dest is None:
            o_ref[:, ob * PROJ_TN:(ob + 1) * PROJ_TN] = y.astype(o_ref.dtype)
            ob += 1
        else:
            t_refs[dest][...] = (y if transposed else y.T).astype(t_refs[dest].dtype)


def _norm_proj(x2d, g, w, ops, gains, tm, gain_t=None):
    n, d = x2d.shape
    n_t = sum(dest is not None for _, _, dest in ops)
    nout = (len(ops) - n_t) * PROJ_TN

    def resident(a):
        return pl.BlockSpec(a.shape, lambda i: (0,) * a.ndim, pipeline_mode=pl.Buffered(1))

    g2 = g.reshape(1, d)
    cast_in_kernel = w.dtype != BF16
    return pl.pallas_call(
        functools.partial(_norm_proj_kernel, ops=ops),
        grid=(n // tm,),
        in_specs=[pl.BlockSpec((tm, d), lambda i: (i, 0)),
                  resident(g2), resident(w), resident(gains)]
                 + ([resident(gain_t)] if gain_t is not None else []),
        out_specs=[pl.BlockSpec((tm, nout), lambda i: (i, 0))]
                  + [pl.BlockSpec((PROJ_TN, tm), lambda i: (0, i))] * n_t,
        out_shape=[jax.ShapeDtypeStruct((n, nout), BF16)] + [jax.ShapeDtypeStruct((PROJ_TN, n), BF16)] * n_t,
        scratch_shapes=[pltpu.VMEM(w.shape, BF16)] if cast_in_kernel else [],
        compiler_params=_cparams(("arbitrary" if cast_in_kernel else "parallel",)),
        name="norm_proj",
    )(x2d, g2, w, gains, *([gain_t] if gain_t is not None else []))


CONV_HALO = 32
CONV_ROWS = 32
CONV_COLS = 512


def _conv_kernel(u_ref, gate_ref, dw_ref, dwb_ref, lng_ref, lnb_ref, o_ref, win_ref, y_ref, *, ts):
    j = pl.program_id(1)
    t0 = pl.multiple_of(j * ts, ts)
    d = u_ref.shape[1]

    @pl.when(j == 0)
    def _():
        win_ref[0, 0:CONV_HALO, :] = jnp.zeros((CONV_HALO, d), F32)

    @pl.when(j > 0)
    def _():
        win_ref[0, 0:CONV_HALO, :] = u_ref[pl.ds(t0 - CONV_HALO, CONV_HALO), :].astype(F32)

    win_ref[0, CONV_HALO:CONV_HALO + ts, :] = u_ref[pl.ds(t0, ts), :].astype(F32)

    first = CONV_HALO - (CONV_K - 1)
    span = CONV_HALO + ts - V7X_SUBLANES
    for c in range(d // V7X_LANES):
        cs = slice(c * V7X_LANES, (c + 1) * V7X_LANES)
        x = win_ref[0, :, cs]
        for s in range(1, V7X_SUBLANES):
            x = pltpu.roll(x, x.shape[0] - 1, 0)
            win_ref[s, 0:span, cs] = x[0:span]

    for r0 in range(0, ts, CONV_ROWS):
        for c in range(d // CONV_COLS):
            cs = slice(c * CONV_COLS, (c + 1) * CONV_COLS)
            acc = jnp.zeros((CONV_ROWS, CONV_COLS), F32)
            for s in range(V7X_SUBLANES):
                taps = [k for k in range(CONV_K) if (first + k) % V7X_SUBLANES == s]
                lo = r0 + first + taps[0] - s
                big = win_ref[s, lo:lo + (taps[-1] - taps[0]) + CONV_ROWS, cs]
                for k in taps:
                    w = jnp.concatenate([dw_ref[k, :, cs]] * (CONV_ROWS // V7X_SUBLANES), axis=0)
                    acc = acc + big[k - taps[0]:k - taps[0] + CONV_ROWS] * w
            y_ref[r0:r0 + CONV_ROWS, cs] = acc + dwb_ref[:, cs]

    y = y_ref[...]
    mu = jnp.mean(y, axis=-1, keepdims=True)
    yc = y - mu
    yn = yc * lax.rsqrt(jnp.mean(yc * yc, axis=-1, keepdims=True) + LN_EPS)
    yn = yn * lng_ref[...] + lnb_ref[...]
    o_ref[...] = (_silu(yn) * gate_ref[...].astype(F32)).astype(o_ref.dtype)


def _conv_branch(acts, u_block, gate_block, dw, dwb, lng, lnb, batch, seq, ts):
    n, d = acts.shape[0], dw.shape[1]
    nt = seq // ts
    return pl.pallas_call(
        functools.partial(_conv_kernel, ts=ts),
        grid=(batch, nt),
        in_specs=[pl.BlockSpec((seq, d), lambda b, j: (b, u_block)),
                  pl.BlockSpec((ts, d), lambda b, j: (b * nt + j, gate_block)),
                  pl.BlockSpec((CONV_K, V7X_SUBLANES, d), lambda b, j: (0, 0, 0)),
                  pl.BlockSpec((1, d), lambda b, j: (0, 0)),
                  pl.BlockSpec((1, d), lambda b, j: (0, 0)),
                  pl.BlockSpec((1, d), lambda b, j: (0, 0))],
        out_specs=pl.BlockSpec((ts, d), lambda b, j: (b * nt + j, 0)),
        out_shape=jax.ShapeDtypeStruct((n, d), BF16),
        scratch_shapes=[pltpu.VMEM((V7X_SUBLANES, CONV_HALO + ts, d), F32), pltpu.VMEM((ts, d), F32)],
        compiler_params=_cparams(("parallel", "arbitrary")),
        name="conv_branch",
    )(acts, acts, jnp.broadcast_to(dw[:, None, :], (CONV_K, V7X_SUBLANES, d)),
      dwb.reshape(1, d), lng.reshape(1, d), lnb.reshape(1, d))


EXTRA_ROWS = 16
PIECES = 3
POS_RADIX = 128
BF16_NORM_MARGIN = 1.02
MAX_FAST_BOUND = 40.0


def _pieces(v):
    out = []
    for _ in range(PIECES):
        piece = v.astype(BF16).astype(F32)
        out.append(piece)
        v = v - piece
    return out


def _position_tables(heads, seq):
    d = DIFF_HEAD_DIM
    rest = (2.0 ** (-8.0 * np.arange(1, heads + 1, dtype=np.float32) / heads) * np.float32(LOG2_E)).astype(np.float32)
    slopes = []
    for _ in range(PIECES):
        piece = rest.astype(BF16).astype(np.float32)
        slopes.append(piece)
        rest = rest - piece
    pos = np.arange(seq)
    hi = np.broadcast_to((pos // POS_RADIX).astype(np.float32), (heads, seq))
    lo = np.broadcast_to((pos % POS_RADIX).astype(np.float32), (heads, seq))
    const = lambda v: np.broadcast_to(v[:, None], (heads, seq))
    zero = np.zeros((heads, seq), np.float32)
    q_rows = ([const(POS_RADIX * p) for p in slopes] + [const(p) for p in slopes] + [hi] * PIECES + [lo] * PIECES
              + [zero] * (PIECES + 1))
    k_rows = ([hi] * PIECES + [lo] * PIECES + [const(-POS_RADIX * p) for p in slopes] + [const(-p) for p in slopes]
              + [zero + 1.0] * PIECES + [zero])
    q_extra = np.stack(q_rows, axis=1)
    k_extra = np.stack(k_rows, axis=2)
    k_tab = np.zeros((heads, 2, seq, DIFF_V_DIM), np.float32)
    k_tab[:, 0, :, d:d + EXTRA_ROWS] = k_extra
    k_tab[:, 1, :, :EXTRA_ROWS] = k_extra
    return jnp.asarray(q_extra.astype(BF16)), jnp.asarray(k_tab.astype(BF16))


def _score_bound(qn_g, kn_g):
    bound = (DIFF_HEAD_DIM ** 0.5 * LOG2_E * BF16_NORM_MARGIN) * jnp.max(jnp.abs(qn_g)) * jnp.max(jnp.abs(kn_g))
    fast = bound < MAX_FAST_BOUND
    neg_b = _pieces(jnp.where(fast, -bound, 0.0))
    rows = [jnp.zeros((), F32)] * (4 * PIECES) + neg_b + [jnp.zeros((), F32)]
    tile = jnp.broadcast_to(jnp.stack(rows)[:, None], (EXTRA_ROWS, V7X_LANES))
    return tile, fast.astype(jnp.int32).reshape(1)


HEADS_PER_STEP = 2


def _diff_attn_kernel(fast_ref, bound_ref, qx_ref, kx_ref, qt_in_ref, k_ref, vt_ref, gate_ref, lam_ref, subg_ref,
                      o_ref, km_ref, qt_ref, mask_ref, *, tq, lambda_init):
    seq = k_ref.shape[0]
    nq = seq // tq
    d = DIFF_HEAD_DIM
    first_half = lax.broadcasted_iota(jnp.int32, (tq, DIFF_V_DIM), 1) < d
    pad = jnp.zeros((DIFF_V_DIM - d - EXTRA_ROWS, tq), qt_ref.dtype)
    bound_rows = jnp.concatenate([bound_ref[...]] * (tq // V7X_LANES), axis=1)

    for hh in range(HEADS_PER_STEP):
        hl = slice(hh * DIFF_V_DIM, (hh + 1) * DIFF_V_DIM)
        for c in range(nq):
            rows = slice(c * tq, (c + 1) * tq)
            k = k_ref[rows, hl]
            km_ref[hh, 0, rows, :] = jnp.where(first_half, k, kx_ref[hh, 0, rows, :])
            km_ref[hh, 1, rows, :] = jnp.where(first_half, kx_ref[hh, 1, rows, :], k)
        for i in range(nq):
            cols = slice(i * tq, (i + 1) * tq)
            extra = (qx_ref[hh, :, cols].astype(F32) + bound_rows).astype(qt_ref.dtype)
            qt_ref[hh, i, 0] = jnp.concatenate([qt_in_ref[hh * DIFF_V_DIM:hh * DIFF_V_DIM + d, cols], extra, pad], axis=0)
            qt_ref[hh, i, 1] = jnp.concatenate([extra, pad, qt_in_ref[hh * DIFF_V_DIM + d:(hh + 1) * DIFF_V_DIM, cols]],
                                               axis=0)

    kk = lax.broadcasted_iota(jnp.int32, (tq, tq), 0)
    qq = lax.broadcasted_iota(jnp.int32, (tq, tq), 1)
    mask_ref[...] = jnp.where(kk <= qq, 0.0, MASK_VALUE)

    lam_v = lam_ref[...]
    lam = (jnp.exp(jnp.sum(lam_v[0:1] * lam_v[1:2], axis=-1, keepdims=True))
           - jnp.exp(jnp.sum(lam_v[2:3] * lam_v[3:4], axis=-1, keepdims=True)) + lambda_init)

    def scores(hh, i):
        keys = (i + 1) * tq
        return [jnp.dot(km_ref[hh, mp, :keys, :], qt_ref[hh, i, mp], preferred_element_type=F32)
                for mp in range(2)]

    def attend(bounded):
        work = [(hh, i) for hh in range(HEADS_PER_STEP) for i in range(nq)]
        x_next = scores(*work[0])
        for n, (hh, i) in enumerate(work):
            x_cur = x_next
            if n + 1 < len(work):
                x_next = scores(*work[n + 1])
            keys = (i + 1) * tq
            vt = vt_ref[hh * DIFF_V_DIM:(hh + 1) * DIFF_V_DIM, :keys]
            heads_out = []
            for x in x_cur:
                x = (jnp.concatenate([x[:keys - tq], x[keys - tq:] + mask_ref[...]], axis=0) if i
                     else x + mask_ref[...])
                p = jnp.exp2(x) if bounded else jnp.exp2(x - jnp.max(x, axis=0, keepdims=True))
                pv = jnp.dot(vt, p.astype(vt.dtype), preferred_element_type=F32)
                heads_out.append(pv / jnp.sum(p, axis=0, keepdims=True))
            rows = slice(i * tq, (i + 1) * tq)
            hl = slice(hh * DIFF_V_DIM, (hh + 1) * DIFF_V_DIM)
            o = heads_out[0] - lam * heads_out[1]
            o = o * lax.rsqrt(jnp.mean(o * o, axis=0, keepdims=True) + RMS_EPS)
            o = o.T * (subg_ref[...] * (1.0 - lambda_init))
            o_ref[rows, hl] = (o * gate_ref[rows, hl].astype(F32)).astype(o_ref.dtype)

    fast = fast_ref[0] == 1
    pl.when(fast)(functools.partial(attend, True))
    pl.when(jnp.logical_not(fast))(functools.partial(attend, False))


def _diff_attention(acts, q_t, v_t, k_block, gate_block, tables, lam_vecs, subg, batch, seq, heads, tq, lambda_init):
    n = acts.shape[0]
    nq = seq // tq
    hps = HEADS_PER_STEP
    q_extra, k_tab, bound_tile, fast = tables
    kb, gb = (b * heads // hps for b in (k_block, gate_block))
    kernel = functools.partial(_diff_attn_kernel, tq=tq, lambda_init=lambda_init)
    head_spec = lambda first: pl.BlockSpec((seq, hps * DIFF_V_DIM), lambda b, h: (b, first + h))
    t_spec = pl.BlockSpec((hps * DIFF_V_DIM, seq), lambda b, h: (h, b))
    return pl.pallas_call(
        kernel,
        grid=(batch, heads // hps),
        in_specs=[pl.BlockSpec(memory_space=pltpu.SMEM),
                  pl.BlockSpec((EXTRA_ROWS, V7X_LANES), lambda b, h: (0, 0)),
                  pl.BlockSpec((hps, EXTRA_ROWS, seq), lambda b, h: (h, 0, 0)),
                  pl.BlockSpec((hps, 2, seq, DIFF_V_DIM), lambda b, h: (h, 0, 0, 0)),
                  t_spec, head_spec(kb), t_spec, head_spec(gb),
                  pl.BlockSpec((4, DIFF_HEAD_DIM), lambda b, h: (0, 0)),
                  pl.BlockSpec((1, DIFF_V_DIM), lambda b, h: (0, 0))],
        out_specs=head_spec(0),
        out_shape=jax.ShapeDtypeStruct((n, heads * DIFF_V_DIM), BF16),
        scratch_shapes=[pltpu.VMEM((hps, 2, seq, DIFF_V_DIM), BF16),
                        pltpu.VMEM((hps, nq, 2, DIFF_V_DIM, tq), BF16),
                        pltpu.VMEM((tq, tq), F32)],
        compiler_params=_cparams(("parallel", "parallel")),
        name="diff_attention",
    )(fast, bound_tile, q_extra, k_tab, q_t, acts, v_t, acts, lam_vecs, subg.reshape(1, DIFF_V_DIM))


def _cross_attend(q_ref, k_ref, v_ref, gate_ref, heads):
    hd = q_ref.shape[1] // heads
    outs = []
    for h in range(heads):
        sl = slice(h * hd, (h + 1) * hd)
        s = _nt_dot(q_ref[:, sl], k_ref[:, sl])
        m = jnp.max(s, axis=-1, keepdims=True)
        p = jnp.exp(s - m)
        l = jnp.sum(p, axis=-1, keepdims=True)
        o = jnp.dot(p.astype(BF16), v_ref[:, sl], preferred_element_type=F32) / l
        outs.append(o * gate_ref[:, sl].astype(F32))
    return jnp.concatenate(outs, axis=1)


def _merge_kernel(x_ref, ca_ref, da_ref, xq_ref, xk_ref, xv_ref, xg_ref, g0_ref, g1_ref, g2_ref,
                  wc_ref, wd_ref, wx_ref, wo_ref, o_ref, wb_ref, *, x_heads):
    @pl.when(pl.program_id(0) == 0)
    def _():
        for c, w_ref in enumerate((wc_ref, wd_ref, wx_ref, wo_ref)):
            wb_ref[c] = w_ref[...].astype(wb_ref.dtype)

    xa = _cross_attend(xq_ref, xk_ref, xv_ref, xg_ref, x_heads).astype(BF16)
    y = g0_ref[...].astype(F32) * jnp.dot(ca_ref[...], wb_ref[0], preferred_element_type=F32)
    y = y + g1_ref[...].astype(F32) * jnp.dot(da_ref[...], wb_ref[1], preferred_element_type=F32)
    y = y + g2_ref[...].astype(F32) * jnp.dot(xa, wb_ref[2], preferred_element_type=F32)
    o_ref[...] = x_ref[...] + jnp.dot(y.astype(BF16), wb_ref[3], preferred_element_type=F32)


def _merge_out(x2d, ca, da, acts, xq_block, xgate_block, gate_block0, mem_kv, wc, wd, wx, wo, seq, mem_len, tm):
    n, d = x2d.shape
    tiles_per_seq = seq // tm
    row = lambda i: (i, 0)
    fixed = lambda i: (0, 0)
    act_spec = pl.BlockSpec((tm, d), row)
    acts_spec = lambda blk: pl.BlockSpec((tm, d), lambda i: (i, blk))
    mem_spec = lambda half: pl.BlockSpec((mem_len, d), lambda i: (i // tiles_per_seq, half))
    w_spec = pl.BlockSpec((d, d), fixed, pipeline_mode=pl.Buffered(1))
    return pl.pallas_call(
        functools.partial(_merge_kernel, x_heads=X_HEADS),
        grid=(n // tm,),
        in_specs=[pl.BlockSpec((tm, d), row), act_spec, act_spec,
                  acts_spec(xq_block), mem_spec(0), mem_spec(1), acts_spec(xgate_block),
                  *[acts_spec(gate_block0 + c) for c in range(N_BRANCH)],
                  w_spec, w_spec, w_spec, w_spec],
        out_specs=pl.BlockSpec((tm, d), row),
        out_shape=jax.ShapeDtypeStruct((n, d), x2d.dtype),
        scratch_shapes=[pltpu.VMEM((4, d, d), BF16)],
        compiler_params=_cparams(("arbitrary",)),
        name="merge_out",
    )(x2d, ca, da, acts, mem_kv, mem_kv, acts, acts, acts, acts, wc, wd, wx, wo)


def _layer(x, mem, l, norm_g, mem_norm_g, w_in, conv_dw, conv_dw_b, conv_ln_g, conv_ln_b, w_conv_proj,
           diff_qn_g, diff_kn_g, lambda_q1, lambda_k1, lambda_q2, lambda_k2, diff_subln_g, w_diff_proj,
           w_mem_kv, x_qn_g, x_kn_g, w_x_proj, w_out):
    batch, seq, d = x.shape
    mem_len = mem.shape[1]
    heads = d // DIFF_V_DIM
    x_head_dim = d // X_HEADS
    assert x_head_dim == V7X_MXU_DIM and d % V7X_MXU_DIM == 0
    n = batch * seq
    x2d = x.reshape(n, d)
    lambda_init = 0.8 - 0.6 * math.exp(-0.3 * l)
    attn_tables = _position_tables(heads, seq) + _score_bound(diff_qn_g, diff_kn_g)

    assert d == PROJ_TN
    ones = jnp.ones((d,), F32)

    ops = (("glu", (0, 1), None), ("silu", (2,), None), ("norm64", (3,), 0), ("norm64", (4,), None),
           ("none", (5,), 1), ("silu", (6,), None), ("norm256", (7,), None), ("silu", (8,), None),
           ("sigmoid", (9,), None), ("sigmoid", (10,), None), ("sigmoid", (11,), None))
    U, C_GATE, D_K, D_GATE, X_Q, X_GATE, MERGE = range(7)
    OP_D_Q, OP_D_K, OP_X_Q = 2, 3, 6
    gain_rows = {OP_D_Q: jnp.tile(diff_qn_g, d // DIFF_HEAD_DIM) * (DIFF_HEAD_DIM ** -0.5 * LOG2_E),
                 OP_D_K: jnp.tile(diff_kn_g, d // DIFF_HEAD_DIM),
                 OP_X_Q: jnp.tile(x_qn_g, X_HEADS) * (x_head_dim ** -0.5)}
    gains = jnp.stack([gain_rows.get(b, ones) for b in range(len(ops))]).reshape(len(ops), 1, d)
    q_gain_t = jnp.broadcast_to(gain_rows[OP_D_Q][:, None], (d, PROJ_ROWS))
    acts, q_t, v_t = _norm_proj(x2d, norm_g, w_in.astype(BF16), ops, gains, tm=PROJ_ROWS, gain_t=q_gain_t)

    mem_ops = (("norm256", (0,), None), ("none", (1,), None))
    mem_gains = jnp.stack([jnp.tile(x_kn_g, X_HEADS), ones]).reshape(2, 1, d)
    mem_kv, = _norm_proj(mem.reshape(batch * mem_len, d), mem_norm_g, w_mem_kv, mem_ops, mem_gains, tm=PROJ_ROWS)

    conv_act = _conv_branch(acts, U, C_GATE, conv_dw, conv_dw_b, conv_ln_g, conv_ln_b, batch, seq,
                            ts=CONV_ROWS_PER_STEP)
    lam_vecs = jnp.stack([lambda_q1, lambda_k1, lambda_q2, lambda_k2])
    diff_act = _diff_attention(acts, q_t, v_t, D_K, D_GATE, attn_tables, lam_vecs, diff_subln_g,
                               batch, seq, heads, tq=ATTN_Q_ROWS, lambda_init=lambda_init)
    out = _merge_out(x2d, conv_act, diff_act, acts, X_Q, X_GATE, MERGE, mem_kv,
                     w_conv_proj, w_diff_proj, w_x_proj, w_out, seq, mem_len, tm=MERGE_ROWS)
    return out.reshape(batch, seq, d)


def kernel(x, mem, norm_g, mem_norm_g, w_in, conv_dw, conv_dw_b, conv_ln_g, conv_ln_b, w_conv_proj, diff_qn_g, diff_kn_g, lambda_q1, lambda_k1, lambda_q2, lambda_k2, diff_subln_g, w_diff_proj, w_mem_kv, x_qn_g, x_kn_g, w_x_proj, w_out):
    params = (norm_g, mem_norm_g, w_in, conv_dw, conv_dw_b, conv_ln_g, conv_ln_b, w_conv_proj, diff_qn_g,
              diff_kn_g, lambda_q1, lambda_k1, lambda_q2, lambda_k2, diff_subln_g, w_diff_proj, w_mem_kv,
              x_qn_g, x_kn_g, w_x_proj, w_out)
    for l in range(norm_g.shape[0]):
        x = _layer(x, mem, l, *(p[l] for p in params))
    return x
```

```python
import functools
import math

import jax
import jax.numpy as jnp
from jax import lax
import numpy as np
from jax.experimental import pallas as pl
from jax.experimental.pallas import tpu as pltpu

CONV_K = 31
DIFF_HEAD_DIM = 64
DIFF_V_DIM = 2 * DIFF_HEAD_DIM
X_HEADS = 4
N_BRANCH = 3
RMS_EPS = 1e-6
LN_EPS = 1e-5
MASK_VALUE = -1e30
LOG2_E = math.log2(math.e)

V7X_LANES = 128
V7X_SUBLANES = 8
V7X_MXU_DIM = 256
V7X_VMEM_LIMIT_BYTES = 56 * 1024 * 1024

PROJ_ROWS = 256
CONV_ROWS_PER_STEP = 256
ATTN_Q_ROWS = 256
MERGE_ROWS = 512

BF16 = jnp.bfloat16
F32 = jnp.float32


def _cparams(sem):
    return pltpu.CompilerParams(dimension_semantics=sem, vmem_limit_bytes=V7X_VMEM_LIMIT_BYTES)


def _sigmoid(x):
    return 1.0 / (1.0 + jnp.exp(-x))


def _silu(x):
    return x * _sigmoid(x)


def _nt_dot(a, b):
    return lax.dot_general(a, b, (((1,), (1,)), ((), ())), preferred_element_type=F32)


PROJ_TN = 1024
W_RESIDENT_F32_BYTES = 16 * 1024 * 1024
W_STREAM_COLS = 512


def _group_rms_t(yt, group):
    groups = yt.reshape(yt.shape[0] // group, group, yt.shape[1])
    groups = groups * lax.rsqrt(jnp.mean(groups * groups, axis=1, keepdims=True) + RMS_EPS)
    return groups.reshape(yt.shape)


def _norm_proj_kernel(x_ref, g_ref, w_ref, gain_ref, *rest, ops):
    rest = list(rest)
    n_t = sum(dest is not None for _, _, dest in ops)
    gain_t_ref = rest.pop(0) if any(kind.startswith("norm") and dest is not None for kind, _, dest in ops) else None
    o_ref = rest.pop(0)
    t_refs, wb = rest[:n_t], rest[n_t:]
    if len(wb) == 1:
        w_f32, (w_ref,) = w_ref, wb

        @pl.when(pl.program_id(0) == 0)
        def _():
            w_ref[...] = w_f32[...].astype(w_ref.dtype)
    elif wb:
        w_hbm, (w_ref, stage_ref, sem) = w_ref, wb
        chunk = stage_ref.shape[2]
        n_chunks = w_ref.shape[1] // chunk

        def fetch(c):
            return pltpu.make_async_copy(w_hbm.at[:, pl.ds(c * chunk, chunk)], stage_ref.at[c % 2], sem.at[c % 2])

        @pl.when(pl.program_id(0) == 0)
        def _():
            fetch(0).start()
            for c in range(n_chunks):
                if c + 1 < n_chunks:
                    fetch(c + 1).start()
                fetch(c).wait()
                w_ref[:, c * chunk:(c + 1) * chunk] = stage_ref[c % 2].astype(w_ref.dtype)

    x = x_ref[...]
    h = (x * lax.rsqrt(jnp.mean(x * x, axis=-1, keepdims=True) + RMS_EPS) * g_ref[...]).astype(BF16)

    def proj(blk):
        return jnp.dot(h, w_ref[:, blk * PROJ_TN:(blk + 1) * PROJ_TN], preferred_element_type=F32)

    ob = 0
    for n, (kind, blks, dest) in enumerate(ops):
        transposed = False
        if kind == "glu":
            y = proj(blks[0]) * _sigmoid(proj(blks[1]))
        elif kind == "silu":
            y = _silu(proj(blks[0]))
        elif kind == "sigmoid":
            y = _sigmoid(proj(blks[0]))
        elif kind in ("norm64", "norm256"):
            y = _group_rms_t(proj(blks[0]).T, DIFF_HEAD_DIM if kind == "norm64" else V7X_MXU_DIM)
            y, transposed = (y.T * gain_ref[n], False) if dest is None else (y * gain_t_ref[...], True)
        else:
            assert kind == "none", kind
            y = proj(blks[0])
        if dest is None:
            o_ref[:, ob * PROJ_TN:(ob + 1) * PROJ_TN] = y.astype(o_ref.dtype)
            ob += 1
        else:
            t_refs[dest][...] = (y if transposed else y.T).astype(t_refs[dest].dtype)


def _norm_proj(x2d, g, w, ops, gains, tm, gain_t=None):
    n, d = x2d.shape
    n_t = sum(dest is not None for _, _, dest in ops)
    nout = (len(ops) - n_t) * PROJ_TN

    def resident(a):
        return pl.BlockSpec(a.shape, lambda i: (0,) * a.ndim, pipeline_mode=pl.Buffered(1))

    g2 = g.reshape(1, d)
    cast_in_kernel = w.dtype != BF16
    stream = cast_in_kernel and w.size * 4 > W_RESIDENT_F32_BYTES
    scratch = [pltpu.VMEM(w.shape, BF16)] if cast_in_kernel else []
    if stream:
        scratch += [pltpu.VMEM((2, d, W_STREAM_COLS), F32), pltpu.SemaphoreType.DMA((2,))]
    return pl.pallas_call(
        functools.partial(_norm_proj_kernel, ops=ops),
        grid=(n // tm,),
        in_specs=[pl.BlockSpec((tm, d), lambda i: (i, 0)),
                  resident(g2), pl.BlockSpec(memory_space=pl.ANY) if stream else resident(w), resident(gains)]
                 + ([resident(gain_t)] if gain_t is not None else []),
        out_specs=[pl.BlockSpec((tm, nout), lambda i: (i, 0))]
                  + [pl.BlockSpec((PROJ_TN, tm), lambda i: (0, i))] * n_t,
        out_shape=[jax.ShapeDtypeStruct((n, nout), BF16)] + [jax.ShapeDtypeStruct((PROJ_TN, n), BF16)] * n_t,
        scratch_shapes=scratch,
        compiler_params=_cparams(("arbitrary" if cast_in_kernel else "parallel",)),
        name="norm_proj",
    )(x2d, g2, w, gains, *([gain_t] if gain_t is not None else []))


CONV_HALO = 32
CONV_ROWS = 32
CONV_COLS = 512


def _conv_kernel(u_ref, gate_ref, dw_ref, dwb_ref, lng_ref, lnb_ref, o_ref, win_ref, y_ref, *, ts):
    j = pl.program_id(1)
    t0 = pl.multiple_of(j * ts, ts)
    d = u_ref.shape[1]

    @pl.when(j == 0)
    def _():
        win_ref[0, 0:CONV_HALO, :] = jnp.zeros((CONV_HALO, d), F32)

    @pl.when(j > 0)
    def _():
        win_ref[0, 0:CONV_HALO, :] = u_ref[pl.ds(t0 - CONV_HALO, CONV_HALO), :].astype(F32)

    win_ref[0, CONV_HALO:CONV_HALO + ts, :] = u_ref[pl.ds(t0, ts), :].astype(F32)

    first = CONV_HALO - (CONV_K - 1)
    span = CONV_HALO + ts - V7X_SUBLANES
    for c in range(d // V7X_LANES):
        cs = slice(c * V7X_LANES, (c + 1) * V7X_LANES)
        x = win_ref[0, :, cs]
        for s in range(1, V7X_SUBLANES):
            x = pltpu.roll(x, x.shape[0] - 1, 0)
            win_ref[s, 0:span, cs] = x[0:span]

    for r0 in range(0, ts, CONV_ROWS):
        for c in range(d // CONV_COLS):
            cs = slice(c * CONV_COLS, (c + 1) * CONV_COLS)
            acc = jnp.zeros((CONV_ROWS, CONV_COLS), F32)
            for s in range(V7X_SUBLANES):
                taps = [k for k in range(CONV_K) if (first + k) % V7X_SUBLANES == s]
                lo = r0 + first + taps[0] - s
                big = win_ref[s, lo:lo + (taps[-1] - taps[0]) + CONV_ROWS, cs]
                for k in taps:
                    w = jnp.concatenate([dw_ref[k, :, cs]] * (CONV_ROWS // V7X_SUBLANES), axis=0)
                    acc = acc + big[k - taps[0]:k - taps[0] + CONV_ROWS] * w
            y_ref[r0:r0 + CONV_ROWS, cs] = acc + dwb_ref[:, cs]

    y = y_ref[...]
    mu = jnp.mean(y, axis=-1, keepdims=True)
    yc = y - mu
    yn = yc * lax.rsqrt(jnp.mean(yc * yc, axis=-1, keepdims=True) + LN_EPS)
    yn = yn * lng_ref[...] + lnb_ref[...]
    o_ref[...] = (_silu(yn) * gate_ref[...].astype(F32)).astype(o_ref.dtype)


def _conv_branch(acts, u_block, gate_block, dw, dwb, lng, lnb, batch, seq, ts):
    n, d = acts.shape[0], dw.shape[1]
    nt = seq // ts
    return pl.pallas_call(
        functools.partial(_conv_kernel, ts=ts),
        grid=(batch, nt),
        in_specs=[pl.BlockSpec((seq, d), lambda b, j: (b, u_block)),
                  pl.BlockSpec((ts, d), lambda b, j: (b * nt + j, gate_block)),
                  pl.BlockSpec((CONV_K, V7X_SUBLANES, d), lambda b, j: (0, 0, 0)),
                  pl.BlockSpec((1, d), lambda b, j: (0, 0)),
                  pl.BlockSpec((1, d), lambda b, j: (0, 0)),
                  pl.BlockSpec((1, d), lambda b, j: (0, 0))],
        out_specs=pl.BlockSpec((ts, d), lambda b, j: (b * nt + j, 0)),
        out_shape=jax.ShapeDtypeStruct((n, d), BF16),
        scratch_shapes=[pltpu.VMEM((V7X_SUBLANES, CONV_HALO + ts, d), F32), pltpu.VMEM((ts, d), F32)],
        compiler_params=_cparams(("parallel", "arbitrary")),
        name="conv_branch",
    )(acts, acts, jnp.broadcast_to(dw[:, None, :], (CONV_K, V7X_SUBLANES, d)),
      dwb.reshape(1, d), lng.reshape(1, d), lnb.reshape(1, d))


EXTRA_ROWS = 16
PIECES = 3
POS_RADIX = 128
BF16_NORM_MARGIN = 1.02
MAX_FAST_BOUND = 40.0


def _pieces(v):
    out = []
    for _ in range(PIECES):
        piece = v.astype(BF16).astype(F32)
        out.append(piece)
        v = v - piece
    return out


def _position_tables(heads, seq):
    d = DIFF_HEAD_DIM
    rest = (2.0 ** (-8.0 * np.arange(1, heads + 1, dtype=np.float32) / heads) * np.float32(LOG2_E)).astype(np.float32)
    slopes = []
    for _ in range(PIECES):
        piece = rest.astype(BF16).astype(np.float32)
        slopes.append(piece)
        rest = rest - piece
    pos = np.arange(seq)
    hi = np.broadcast_to((pos // POS_RADIX).astype(np.float32), (heads, seq))
    lo = np.broadcast_to((pos % POS_RADIX).astype(np.float32), (heads, seq))
    const = lambda v: np.broadcast_to(v[:, None], (heads, seq))
    zero = np.zeros((heads, seq), np.float32)
    q_rows = ([const(POS_RADIX * p) for p in slopes] + [const(p) for p in slopes] + [hi] * PIECES + [lo] * PIECES
              + [zero] * (PIECES + 1))
    k_rows = ([hi] * PIECES + [lo] * PIECES + [const(-POS_RADIX * p) for p in slopes] + [const(-p) for p in slopes]
              + [zero + 1.0] * PIECES + [zero])
    q_extra = np.stack(q_rows, axis=1)
    k_extra = np.stack(k_rows, axis=2)
    k_tab = np.zeros((heads, 2, seq, DIFF_V_DIM), np.float32)
    k_tab[:, 0, :, d:d + EXTRA_ROWS] = k_extra
    k_tab[:, 1, :, :EXTRA_ROWS] = k_extra
    return jnp.asarray(q_extra.astype(BF16)), jnp.asarray(k_tab.astype(BF16))


def _score_bound(qn_g, kn_g):
    bound = (DIFF_HEAD_DIM ** 0.5 * LOG2_E * BF16_NORM_MARGIN) * jnp.max(jnp.abs(qn_g)) * jnp.max(jnp.abs(kn_g))
    fast = bound < MAX_FAST_BOUND
    neg_b = _pieces(jnp.where(fast, -bound, 0.0))
    rows = [jnp.zeros((), F32)] * (4 * PIECES) + neg_b + [jnp.zeros((), F32)]
    tile = jnp.broadcast_to(jnp.stack(rows)[:, None], (EXTRA_ROWS, V7X_LANES))
    return tile, fast.astype(jnp.int32).reshape(1)


HEADS_PER_STEP = 2


def _diff_attn_kernel(fast_ref, bound_ref, qx_ref, kx_ref, qt_in_ref, k_ref, vt_ref, gate_ref, lam_ref, subg_ref,
                      o_ref, km_ref, qt_ref, mask_ref, *, tq, lambda_init):
    seq = k_ref.shape[0]
    nq = seq // tq
    d = DIFF_HEAD_DIM
    first_half = lax.broadcasted_iota(jnp.int32, (tq, DIFF_V_DIM), 1) < d
    pad = jnp.zeros((DIFF_V_DIM - d - EXTRA_ROWS, tq), qt_ref.dtype)
    bound_rows = jnp.concatenate([bound_ref[...]] * (tq // V7X_LANES), axis=1)

    for hh in range(HEADS_PER_STEP):
        hl = slice(hh * DIFF_V_DIM, (hh + 1) * DIFF_V_DIM)
        for c in range(nq):
            rows = slice(c * tq, (c + 1) * tq)
            k = k_ref[rows, hl]
            km_ref[hh, 0, rows, :] = jnp.where(first_half, k, kx_ref[hh, 0, rows, :])
            km_ref[hh, 1, rows, :] = jnp.where(first_half, kx_ref[hh, 1, rows, :], k)
        for i in range(nq):
            cols = slice(i * tq, (i + 1) * tq)
            extra = (qx_ref[hh, :, cols].astype(F32) + bound_rows).astype(qt_ref.dtype)
            qt_ref[hh, i, 0] = jnp.concatenate([qt_in_ref[hh * DIFF_V_DIM:hh * DIFF_V_DIM + d, cols], extra, pad], axis=0)
            qt_ref[hh, i, 1] = jnp.concatenate([extra, pad, qt_in_ref[hh * DIFF_V_DIM + d:(hh + 1) * DIFF_V_DIM, cols]],
                                               axis=0)

    kk = lax.broadcasted_iota(jnp.int32, (tq, tq), 0)
    qq = lax.broadcasted_iota(jnp.int32, (tq, tq), 1)
    mask_ref[...] = jnp.where(kk <= qq, 0.0, MASK_VALUE)

    lam_v = lam_ref[...]
    lam = (jnp.exp(jnp.sum(lam_v[0:1] * lam_v[1:2], axis=-1, keepdims=True))
           - jnp.exp(jnp.sum(lam_v[2:3] * lam_v[3:4], axis=-1, keepdims=True)) + lambda_init)

    def scores(hh, i):
        keys = (i + 1) * tq
        return [jnp.dot(km_ref[hh, mp, :keys, :], qt_ref[hh, i, mp], preferred_element_type=F32)
                for mp in range(2)]

    def attend(bounded):
        work = [(hh, i) for hh in range(HEADS_PER_STEP) for i in range(nq)]
        x_next = scores(*work[0])
        for n, (hh, i) in enumerate(work):
            x_cur = x_next
            if n + 1 < len(work):
                x_next = scores(*work[n + 1])
            keys = (i + 1) * tq
            vt = vt_ref[hh * DIFF_V_DIM:(hh + 1) * DIFF_V_DIM, :keys]
            heads_out = []
            for x in x_cur:
                x = (jnp.concatenate([x[:keys - tq], x[keys - tq:] + mask_ref[...]], axis=0) if i
                     else x + mask_ref[...])
                p = jnp.exp2(x) if bounded else jnp.exp2(x - jnp.max(x, axis=0, keepdims=True))
                pv = jnp.dot(vt, p.astype(vt.dtype), preferred_element_type=F32)
                heads_out.append(pv / jnp.sum(p, axis=0, keepdims=True))
            rows = slice(i * tq, (i + 1) * tq)
            hl = slice(hh * DIFF_V_DIM, (hh + 1) * DIFF_V_DIM)
            o = heads_out[0] - lam * heads_out[1]
            o = o * lax.rsqrt(jnp.mean(o * o, axis=0, keepdims=True) + RMS_EPS)
            o = o.T * (subg_ref[...] * (1.0 - lambda_init))
            o_ref[rows, hl] = (o * gate_ref[rows, hl].astype(F32)).astype(o_ref.dtype)

    fast = fast_ref[0] == 1
    pl.when(fast)(functools.partial(attend, True))
    pl.when(jnp.logical_not(fast))(functools.partial(attend, False))


def _diff_attention(acts, q_t, v_t, k_block, gate_block, tables, lam_vecs, subg, batch, seq, heads, tq, lambda_init):
    n = acts.shape[0]
    nq = seq // tq
    hps = HEADS_PER_STEP
    q_extra, k_tab, bound_tile, fast = tables
    kb, gb = (b * heads // hps for b in (k_block, gate_block))
    kernel = functools.partial(_diff_attn_kernel, tq=tq, lambda_init=lambda_init)
    head_spec = lambda first: pl.BlockSpec((seq, hps * DIFF_V_DIM), lambda b, h: (b, first + h))
    t_spec = pl.BlockSpec((hps * DIFF_V_DIM, seq), lambda b, h: (h, b))
    return pl.pallas_call(
        kernel,
        grid=(batch, heads // hps),
        in_specs=[pl.BlockSpec(memory_space=pltpu.SMEM),
                  pl.BlockSpec((EXTRA_ROWS, V7X_LANES), lambda b, h: (0, 0)),
                  pl.BlockSpec((hps, EXTRA_ROWS, seq), lambda b, h: (h, 0, 0)),
                  pl.BlockSpec((hps, 2, seq, DIFF_V_DIM), lambda b, h: (h, 0, 0, 0)),
                  t_spec, head_spec(kb), t_spec, head_spec(gb),
                  pl.BlockSpec((4, DIFF_HEAD_DIM), lambda b, h: (0, 0)),
                  pl.BlockSpec((1, DIFF_V_DIM), lambda b, h: (0, 0))],
        out_specs=head_spec(0),
        out_shape=jax.ShapeDtypeStruct((n, heads * DIFF_V_DIM), BF16),
        scratch_shapes=[pltpu.VMEM((hps, 2, seq, DIFF_V_DIM), BF16),
                        pltpu.VMEM((hps, nq, 2, DIFF_V_DIM, tq), BF16),
                        pltpu.VMEM((tq, tq), F32)],
        compiler_params=_cparams(("parallel", "parallel")),
        name="diff_attention",
    )(fast, bound_tile, q_extra, k_tab, q_t, acts, v_t, acts, lam_vecs, subg.reshape(1, DIFF_V_DIM))


def _cross_attend(q_ref, k_ref, v_ref, gate_ref, heads):
    hd = q_ref.shape[1] // heads
    outs = []
    for h in range(heads):
        sl = slice(h * hd, (h + 1) * hd)
        s = _nt_dot(q_ref[:, sl], k_ref[:, sl])
        m = jnp.max(s, axis=-1, keepdims=True)
        p = jnp.exp(s - m)
        l = jnp.sum(p, axis=-1, keepdims=True)
        o = jnp.dot(p.astype(BF16), v_ref[:, sl], preferred_element_type=F32) / l
        outs.append(o * gate_ref[:, sl].astype(F32))
    return jnp.concatenate(outs, axis=1)


def _merge_kernel(x_ref, ca_ref, da_ref, xq_ref, xk_ref, xv_ref, xg_ref, g0_ref, g1_ref, g2_ref,
                  wc_ref, wd_ref, wx_ref, wo_ref, o_ref, wb_ref, *, x_heads):
    @pl.when(pl.program_id(0) == 0)
    def _():
        for c, w_ref in enumerate((wc_ref, wd_ref, wx_ref, wo_ref)):
            wb_ref[c] = w_ref[...].astype(wb_ref.dtype)

    xa = _cross_attend(xq_ref, xk_ref, xv_ref, xg_ref, x_heads).astype(BF16)
    y = g0_ref[...].astype(F32) * jnp.dot(ca_ref[...], wb_ref[0], preferred_element_type=F32)
    y = y + g1_ref[...].astype(F32) * jnp.dot(da_ref[...], wb_ref[1], preferred_element_type=F32)
    y = y + g2_ref[...].astype(F32) * jnp.dot(xa, wb_ref[2], preferred_element_type=F32)
    o_ref[...] = x_ref[...] + jnp.dot(y.astype(BF16), wb_ref[3], preferred_element_type=F32)


def _merge_out(x2d, ca, da, acts, xq_block, xgate_block, gate_block0, mem_kv, wc, wd, wx, wo, seq, mem_len, tm):
    n, d = x2d.shape
    tiles_per_seq = seq // tm
    row = lambda i: (i, 0)
    fixed = lambda i: (0, 0)
    act_spec = pl.BlockSpec((tm, d), row)
    acts_spec = lambda blk: pl.BlockSpec((tm, d), lambda i: (i, blk))
    mem_spec = lambda half: pl.BlockSpec((mem_len, d), lambda i: (i // tiles_per_seq, half))
    w_spec = pl.BlockSpec((d, d), fixed, pipeline_mode=pl.Buffered(1))
    return pl.pallas_call(
        functools.partial(_merge_kernel, x_heads=X_HEADS),
        grid=(n // tm,),
        in_specs=[pl.BlockSpec((tm, d), row), act_spec, act_spec,
                  acts_spec(xq_block), mem_spec(0), mem_spec(1), acts_spec(xgate_block),
                  *[acts_spec(gate_block0 + c) for c in range(N_BRANCH)],
                  w_spec, w_spec, w_spec, w_spec],
        out_specs=pl.BlockSpec((tm, d), row),
        out_shape=jax.ShapeDtypeStruct((n, d), x2d.dtype),
        scratch_shapes=[pltpu.VMEM((4, d, d), BF16)],
        compiler_params=_cparams(("arbitrary",)),
        name="merge_out",
    )(x2d, ca, da, acts, mem_kv, mem_kv, acts, acts, acts, acts, wc, wd, wx, wo)


def _layer(x, mem, l, norm_g, mem_norm_g, w_in, conv_dw, conv_dw_b, conv_ln_g, conv_ln_b, w_conv_proj,
           diff_qn_g, diff_kn_g, lambda_q1, lambda_k1, lambda_q2, lambda_k2, diff_subln_g, w_diff_proj,
           w_mem_kv, x_qn_g, x_kn_g, w_x_proj, w_out):
    batch, seq, d = x.shape
    mem_len = mem.shape[1]
    heads = d // DIFF_V_DIM
    x_head_dim = d // X_HEADS
    assert x_head_dim == V7X_MXU_DIM and d % V7X_MXU_DIM == 0
    n = batch * seq
    x2d = x.reshape(n, d)
    lambda_init = 0.8 - 0.6 * math.exp(-0.3 * l)
    attn_tables = _position_tables(heads, seq) + _score_bound(diff_qn_g, diff_kn_g)

    assert d == PROJ_TN
    ones = jnp.ones((d,), F32)

    ops = (("glu", (0, 1), None), ("silu", (2,), None), ("norm64", (3,), 0), ("norm64", (4,), None),
           ("none", (5,), 1), ("silu", (6,), None), ("norm256", (7,), None), ("silu", (8,), None),
           ("sigmoid", (9,), None), ("sigmoid", (10,), None), ("sigmoid", (11,), None))
    U, C_GATE, D_K, D_GATE, X_Q, X_GATE, MERGE = range(7)
    OP_D_Q, OP_D_K, OP_X_Q = 2, 3, 6
    gain_rows = {OP_D_Q: jnp.tile(diff_qn_g, d // DIFF_HEAD_DIM) * (DIFF_HEAD_DIM ** -0.5 * LOG2_E),
                 OP_D_K: jnp.tile(diff_kn_g, d // DIFF_HEAD_DIM),
                 OP_X_Q: jnp.tile(x_qn_g, X_HEADS) * (x_head_dim ** -0.5)}
    gains = jnp.stack([gain_rows.get(b, ones) for b in range(len(ops))]).reshape(len(ops), 1, d)
    q_gain_t = jnp.broadcast_to(gain_rows[OP_D_Q][:, None], (d, PROJ_ROWS))
    acts, q_t, v_t = _norm_proj(x2d, norm_g, w_in, ops, gains, tm=PROJ_ROWS, gain_t=q_gain_t)

    mem_ops = (("norm256", (0,), None), ("none", (1,), None))
    mem_gains = jnp.stack([jnp.tile(x_kn_g, X_HEADS), ones]).reshape(2, 1, d)
    mem_kv, = _norm_proj(mem.reshape(batch * mem_len, d), mem_norm_g, w_mem_kv, mem_ops, mem_gains, tm=PROJ_ROWS)

    conv_act = _conv_branch(acts, U, C_GATE, conv_dw, conv_dw_b, conv_ln_g, conv_ln_b, batch, seq,
                            ts=CONV_ROWS_PER_STEP)
    lam_vecs = jnp.stack([lambda_q1, lambda_k1, lambda_q2, lambda_k2])
    diff_act = _diff_attention(acts, q_t, v_t, D_K, D_GATE, attn_tables, lam_vecs, diff_subln_g,
                               batch, seq, heads, tq=ATTN_Q_ROWS, lambda_init=lambda_init)
    out = _merge_out(x2d, conv_act, diff_act, acts, X_Q, X_GATE, MERGE, mem_kv,
                     w_conv_proj, w_diff_proj, w_x_proj, w_out, seq, mem_len, tm=MERGE_ROWS)
    return out.reshape(batch, seq, d)


def kernel(x, mem, norm_g, mem_norm_g, w_in, conv_dw, conv_dw_b, conv_ln_g, conv_ln_b, w_conv_proj, diff_qn_g, diff_kn_g, lambda_q1, lambda_k1, lambda_q2, lambda_k2, diff_subln_g, w_diff_proj, w_mem_kv, x_qn_g, x_kn_g, w_x_proj, w_out):
    params = (norm_g, mem_norm_g, w_in, conv_dw, conv_dw_b, conv_ln_g, conv_ln_b, w_conv_proj, diff_qn_g,
              diff_kn_g, lambda_q1, lambda_k1, lambda_q2, lambda_k2, diff_subln_g, w_diff_proj, w_mem_kv,
              x_qn_g, x_kn_g, w_x_proj, w_out)
    for l in range(norm_g.shape[0]):
        x = _layer(x, mem, l, *(p[l] for p in params))
    return x
```
